```python
import math
import jax, jax.numpy as jnp
from jax import lax
import numpy as np

D_MODEL = 1024
BATCH = 4
SEQ = 4096
DEPTH = 1
DEC_BATCH = 16
DEC_SEQ = 4096
PAST_LEN = 128

DA_HEADS = 8
DA_HEAD_DIM = 64
DA_QK = DA_HEADS * 2 * DA_HEAD_DIM
DA_V = DA_HEADS * 2 * DA_HEAD_DIM
Q_BLOCK = 128
DIL_GROUPS = ((128, 1), (512, 4), (2048, 16))
DIL_HEADS = 8
DIL_HEAD_DIM = 64
DIL_QKV = len(DIL_GROUPS) * DIL_HEADS * DIL_HEAD_DIM
DIL_OUT = DIL_HEADS * DIL_HEAD_DIM
IN_COLS = 2 * DA_QK + DA_V + 3 * DIL_QKV + 2 * D_MODEL
NUM_BUCKETS = 32
MAX_DISTANCE = 1024
N_REL_HEADS = DA_HEADS + len(DIL_GROUPS) * DIL_HEADS
PEER_HEADS = 8
N_KEYS = 128
N_EXPERTS = N_KEYS * N_KEYS
PEER_QUERY_DIM = 256
PEER_HALF = PEER_QUERY_DIM // 2
PEER_TOPK = 16
TOK_CHUNK = 128
EPS = 1e-6
NEG_BIG = -1e30

kernel_name = 'hybrid_diffattn_dilated_peer_encoder'


def rms_norm(x, g):
    xf = x.astype(jnp.float32)
    y = xf * lax.rsqrt(jnp.mean(xf * xf, axis=-1, keepdims=True) + EPS)
    return (y * g.astype(jnp.float32)).astype(x.dtype)


def rel_bucket(rel):
    half = NUM_BUCKETS // 2
    max_exact = half // 2
    ret = jnp.where(rel > 0, half, 0).astype(jnp.int32)
    n = jnp.abs(rel).astype(jnp.int32)
    nf = jnp.maximum(n, max_exact).astype(jnp.float32)
    large = max_exact + (jnp.log(nf / max_exact) / math.log(MAX_DISTANCE / max_exact)
                         * (half - max_exact)).astype(jnp.int32)
    large = jnp.minimum(large, half - 1)
    return ret + jnp.where(n < max_exact, n, large)


def diff_attention(q, k, v, lam, bias_table):
    B, S, H, _, d = q.shape
    nb = S // Q_BLOCK
    scale = d ** -0.5
    k1, k2 = k[:, :, :, 0], k[:, :, :, 1]
    qb = q.reshape(B, nb, Q_BLOCK, H, 2, d).transpose(1, 0, 2, 3, 4, 5)
    starts = jnp.arange(nb, dtype=jnp.int32) * Q_BLOCK
    kpos = jnp.arange(S, dtype=jnp.int32)

    def one_block(args):
        qblk, start = args
        qpos = start + jnp.arange(Q_BLOCK, dtype=jnp.int32)
        bias = bias_table[rel_bucket(kpos[None, :] - qpos[:, None])]
        bias = bias.astype(jnp.float32).transpose(2, 0, 1)[None]
        s1 = jnp.einsum('bqhd,bkhd->bhqk', qblk[:, :, :, 0], k1, preferred_element_type=jnp.float32) * scale + bias
        s2 = jnp.einsum('bqhd,bkhd->bhqk', qblk[:, :, :, 1], k2, preferred_element_type=jnp.float32) * scale + bias
        attn = jax.nn.softmax(s1, axis=-1) - lam * jax.nn.softmax(s2, axis=-1)
        return jnp.einsum('bhqk,bkhe->bqhe', attn.astype(v.dtype), v, preferred_element_type=jnp.float32)

    out = lax.map(one_block, (qb, starts))
    return out.transpose(1, 0, 2, 3, 4).reshape(B, S, H, v.shape[-1])


def dilated_group(q, k, v, window, dilation, bias_table):
    B, S, H, d = q.shape
    r = dilation
    hs = window // (2 * r)
    L = S // r
    n = -(-L // hs)
    Lp = n * hs
    scale = d ** -0.5

    def sub(t):
        return t.reshape(B, L, r, H, d).transpose(0, 2, 1, 3, 4)

    qs = jnp.pad(sub(q), ((0, 0), (0, 0), (0, Lp - L), (0, 0), (0, 0))).reshape(B, r, n, hs, H, d)
    pad_kv = ((0, 0), (0, 0), (hs, Lp - L + hs), (0, 0), (0, 0))
    kp = jnp.pad(sub(k), pad_kv).reshape(B, r, n + 2, hs, H, d)
    vp = jnp.pad(sub(v), pad_kv).reshape(B, r, n + 2, hs, H, d)
    kw = jnp.concatenate([kp[:, :, :-2], kp[:, :, 1:-1], kp[:, :, 2:]], axis=3)
    vw = jnp.concatenate([vp[:, :, :-2], vp[:, :, 1:-1], vp[:, :, 2:]], axis=3)
    rel_sub = jnp.arange(3 * hs, dtype=jnp.int32)[None, :] - hs - jnp.arange(hs, dtype=jnp.int32)[:, None]
    bias = bias_table[rel_bucket(r * rel_sub)].astype(jnp.float32).transpose(2, 0, 1)
    kj = jnp.arange(n, dtype=jnp.int32)[:, None] * hs + jnp.arange(3 * hs, dtype=jnp.int32)[None, :] - hs
    valid = (jnp.abs(rel_sub)[None] <= hs) & (kj[:, None, :] >= 0) & (kj[:, None, :] < L)
    s = jnp.einsum('brnqhd,brnkhd->brnhqk', qs, kw, preferred_element_type=jnp.float32) * scale + bias
    s = jnp.where(valid[None, None, :, None], s, NEG_BIG)
    m = jnp.max(s, axis=-1, keepdims=True)
    p = jnp.exp(s - m)
    den = jnp.sum(p, axis=-1, keepdims=True)
    o = jnp.einsum('brnhqk,brnkhd->brnhqd', p.astype(v.dtype), vw, preferred_element_type=jnp.float32) / den
    lse = (m + jnp.log(den))[..., 0]
    o = o.transpose(0, 1, 2, 4, 3, 5).reshape(B, r, Lp, H, d)[:, :, :L]
    o = o.transpose(0, 2, 1, 3, 4).reshape(B, S, H, d)
    lse = lse.transpose(0, 1, 2, 4, 3).reshape(B, r, Lp, H)[:, :, :L]
    lse = lse.transpose(0, 2, 1, 3).reshape(B, S, H)
    return o, lse


def token_mixers(h, layer, w_in, w_proj_a, w_proj_b, w_out,
                 lambda_q1, lambda_k1, lambda_q2, lambda_k2, g_subln, rel_bias):
    B, S, _ = h.shape
    p = h @ w_in
    cuts = np.cumsum([DA_QK, DA_QK, DA_V, DIL_QKV, DIL_QKV, DIL_QKV, D_MODEL]).tolist()
    qa, ka, va, qb, kb, vb, ga, gb = jnp.split(p, cuts, axis=-1)
    lam_init = 0.8 - 0.6 * math.exp(-0.3 * layer)
    lam = (jnp.exp(jnp.sum(lambda_q1.astype(jnp.float32) * lambda_k1.astype(jnp.float32)))
           - jnp.exp(jnp.sum(lambda_q2.astype(jnp.float32) * lambda_k2.astype(jnp.float32))) + lam_init)
    oa = diff_attention(qa.reshape(B, S, DA_HEADS, 2, DA_HEAD_DIM), ka.reshape(B, S, DA_HEADS, 2, DA_HEAD_DIM),
                        va.reshape(B, S, DA_HEADS, 2 * DA_HEAD_DIM), lam, rel_bias[:, :DA_HEADS])
    oa = rms_norm(oa, g_subln) * (1.0 - lam_init)
    oa = oa.reshape(B, S, DA_V).astype(h.dtype) @ w_proj_a
    G = len(DIL_GROUPS)
    qb = qb.reshape(B, S, G, DIL_HEADS, DIL_HEAD_DIM)
    kb = kb.reshape(B, S, G, DIL_HEADS, DIL_HEAD_DIM)
    vb = vb.reshape(B, S, G, DIL_HEADS, DIL_HEAD_DIM)
    outs, lses = [], []
    for gi, (window, dilation) in enumerate(DIL_GROUPS):
        cols = rel_bias[:, DA_HEADS + gi * DIL_HEADS: DA_HEADS + (gi + 1) * DIL_HEADS]
        o_g, lse_g = dilated_group(qb[:, :, gi], kb[:, :, gi], vb[:, :, gi], window, dilation, cols)
        outs.append(o_g)
        lses.append(lse_g)
    wts = jax.nn.softmax(jnp.stack(lses, axis=-1), axis=-1)
    ob = jnp.sum(jnp.stack(outs, axis=-1) * wts[:, :, :, None, :], axis=-1)
    ob = ob.reshape(B, S, DIL_OUT).astype(h.dtype) @ w_proj_b
    merged = jax.nn.sigmoid(ga) * oa + jax.nn.sigmoid(gb) * ob
    return merged @ w_out


def peer(h, w_query, sub_keys_1, sub_keys_2, expert_u, expert_v):
    B, S, D = h.shape
    xt = h.reshape((B * S) // TOK_CHUNK, TOK_CHUNK, D)

    def chunk(xc):
        q = (xc @ w_query).reshape(TOK_CHUNK, PEER_HEADS, 2, PEER_HALF)
        s1 = jnp.einsum('thd,hkd->thk', q[:, :, 0], sub_keys_1, preferred_element_type=jnp.float32)
        s2 = jnp.einsum('thd,hkd->thk', q[:, :, 1], sub_keys_2, preferred_element_type=jnp.float32)
        v1, i1 = lax.top_k(s1, PEER_TOPK)
        v2, i2 = lax.top_k(s2, PEER_TOPK)
        cand = (v1[..., :, None] + v2[..., None, :]).reshape(TOK_CHUNK, PEER_HEADS, PEER_TOPK * PEER_TOPK)
        vals, ids = lax.top_k(cand, PEER_TOPK)
        e1 = jnp.take_along_axis(i1, ids // PEER_TOPK, axis=-1)
        e2 = jnp.take_along_axis(i2, ids % PEER_TOPK, axis=-1)
        experts = e1 * N_KEYS + e2
        g = jax.nn.softmax(vals, axis=-1)
        u = jnp.take(expert_u, experts, axis=0)
        vv = jnp.take(expert_v, experts, axis=0)
        a = jax.nn.gelu(jnp.einsum('td,thkd->thk', xc, u, preferred_element_type=jnp.float32), approximate=False) * g
        return jnp.einsum('thk,thkd->td', a.astype(vv.dtype), vv, preferred_element_type=jnp.float32)

    return lax.map(chunk, xt).reshape(B, S, D)


def encoder(x, c, w_ada, b_ada, g_norm_mix, g_norm_ffn, w_in, w_proj_a, w_proj_b, w_out,
            lambda_q1, lambda_k1, lambda_q2, lambda_k2, g_subln, rel_bias,
            w_query, sub_keys_1, sub_keys_2, expert_u, expert_v, g_final):
    B = x.shape[0]
    for l in range(DEPTH):
        mod = (jax.nn.silu(c) @ w_ada[l] + b_ada[l]).reshape(B, 1, 6, D_MODEL)
        shift1, scale1, gate1 = mod[:, :, 0], mod[:, :, 1], mod[:, :, 2]
        shift2, scale2, gate2 = mod[:, :, 3], mod[:, :, 4], mod[:, :, 5]
        h = rms_norm(x, g_norm_mix[l]) * (1.0 + scale1) + shift1
        mix = token_mixers(h, l, w_in[l], w_proj_a[l], w_proj_b[l], w_out[l],
                           lambda_q1[l], lambda_k1[l], lambda_q2[l], lambda_k2[l], g_subln[l], rel_bias)
        x = x + (gate1 * mix).astype(x.dtype)
        h = rms_norm(x, g_norm_ffn[l]) * (1.0 + scale2) + shift2
        ff = peer(h, w_query[l], sub_keys_1[l], sub_keys_2[l], expert_u[l], expert_v[l])
        x = x + (gate2 * ff).astype(x.dtype)
    return rms_norm(x, g_final)


def setup_inputs(seed: int = 0) -> dict:
    key = jax.random.key(seed)
    ks = jax.random.split(key, 26)

    def nrm(k, shape, scale):
        return jax.random.normal(k, shape, jnp.float32) * scale

    D = D_MODEL
    return {
        'x_prompt': nrm(ks[0], (BATCH, SEQ, D), 1.0),
        'x_sample': nrm(ks[1], (DEC_BATCH, DEC_SEQ, D), 1.0),
        'c_prompt': nrm(ks[2], (BATCH, D), 1.0),
        'c_sample': nrm(ks[3], (DEC_BATCH, D), 1.0),
        'w_ada': nrm(ks[4], (DEPTH, D, 6 * D), 0.5 * D ** -0.5),
        'b_ada': nrm(ks[5], (DEPTH, 6 * D), 0.01),
        'g_norm_mix': 1.0 + nrm(ks[6], (DEPTH, D), 0.02),
        'g_norm_ffn': 1.0 + nrm(ks[7], (DEPTH, D), 0.02),
        'w_in': nrm(ks[8], (DEPTH, D, IN_COLS), D ** -0.5),
        'w_proj_a': nrm(ks[9], (DEPTH, DA_V, D), DA_V ** -0.5),
        'w_proj_b': nrm(ks[10], (DEPTH, DIL_OUT, D), DIL_OUT ** -0.5),
        'w_out': nrm(ks[11], (DEPTH, D, D), D ** -0.5),
        'lambda_q1': nrm(ks[12], (DEPTH, DA_HEAD_DIM), 0.1),
        'lambda_k1': nrm(ks[13], (DEPTH, DA_HEAD_DIM), 0.1),
        'lambda_q2': nrm(ks[14], (DEPTH, DA_HEAD_DIM), 0.1),
        'lambda_k2': nrm(ks[15], (DEPTH, DA_HEAD_DIM), 0.1),
        'g_subln': 1.0 + nrm(ks[16], (DEPTH, 2 * DA_HEAD_DIM), 0.02),
        'rel_bias': nrm(ks[17], (NUM_BUCKETS, N_REL_HEADS), 0.5),
        'w_query': nrm(ks[18], (DEPTH, D, PEER_HEADS * PEER_QUERY_DIM), D ** -0.5),
        'sub_keys_1': nrm(ks[19], (DEPTH, PEER_HEADS, N_KEYS, PEER_HALF), PEER_HALF ** -0.5),
        'sub_keys_2': nrm(ks[20], (DEPTH, PEER_HEADS, N_KEYS, PEER_HALF), PEER_HALF ** -0.5),
        'expert_u': nrm(ks[21], (DEPTH, N_EXPERTS, D), D ** -0.5),
        'expert_v': nrm(ks[22], (DEPTH, N_EXPERTS, D), 0.5),
        'g_final': 1.0 + nrm(ks[23], (D,), 0.02),
    }


def reference(x_prompt, x_sample, c_prompt, c_sample, w_ada, b_ada, g_norm_mix, g_norm_ffn,
              w_in, w_proj_a, w_proj_b, w_out, lambda_q1, lambda_k1, lambda_q2, lambda_k2,
              g_subln, rel_bias, w_query, sub_keys_1, sub_keys_2, expert_u, expert_v, g_final):
    y_prompt = encoder(x_prompt, c_prompt, w_ada, b_ada, g_norm_mix, g_norm_ffn, w_in, w_proj_a, w_proj_b, w_out,
                       lambda_q1, lambda_k1, lambda_q2, lambda_k2, g_subln, rel_bias,
                       w_query, sub_keys_1, sub_keys_2, expert_u, expert_v, g_final)
    y_sample = encoder(x_sample, c_sample, w_ada, b_ada, g_norm_mix, g_norm_ffn, w_in, w_proj_a, w_proj_b, w_out,
                       lambda_q1, lambda_k1, lambda_q2, lambda_k2, g_subln, rel_bias,
                       w_query, sub_keys_1, sub_keys_2, expert_u, expert_v, g_final)
    return (y_prompt, y_sample)
```

```python
import functools
import math

import jax
import jax.numpy as jnp
import numpy as np
from jax import lax
from jax.experimental import pallas as pl
from jax.experimental.pallas import tpu as pltpu

F32 = jnp.float32
BF16 = jnp.bfloat16

D_MODEL = 1024
DA_HEADS = 8
DA_HEAD_DIM = 64
DA_QK = DA_HEADS * 2 * DA_HEAD_DIM
DA_V = DA_HEADS * 2 * DA_HEAD_DIM
DIL_GROUPS = ((128, 1), (512, 4), (2048, 16))
DIL_HEADS = 8
DIL_HEAD_DIM = 64
DIL_HALF_SPAN = 64
DIL_COLS = DIL_HEADS * DIL_HEAD_DIM
DIL_QKV = len(DIL_GROUPS) * DIL_COLS
IN_COLS = 2 * DA_QK + DA_V + 3 * DIL_QKV + 2 * D_MODEL
NUM_BUCKETS = 32
N_REL_HEADS = DA_HEADS + len(DIL_GROUPS) * DIL_HEADS
PEER_HEADS = 8
N_KEYS = 128
N_EXPERTS = N_KEYS * N_KEYS
PEER_QUERY_DIM = 256
PEER_HALF = PEER_QUERY_DIM // 2
PEER_TOPK = 16
EPS = 1e-6
NEG_BIG = -1e30
LAM_INIT = 0.8 - 0.6 * math.exp(-0.3 * 0)

LANES = 128
VMEM_LIMIT = 56 * 1024 * 1024

OFF_QA, OFF_KA, OFF_VA = 0, DA_QK, 2 * DA_QK
OFF_QB = 2 * DA_QK + DA_V
OFF_KB = OFF_QB + DIL_QKV
OFF_VB = OFF_KB + DIL_QKV
OFF_GA = OFF_VB + DIL_QKV
OFF_GB = OFF_GA + D_MODEL

BUCKET_THRESHOLDS = (1, 2, 3, 4, 5, 6, 7, 8, 15, 27, 50, 91, 166, 305, 559)
BUCKET_CONST_FROM = BUCKET_THRESHOLDS[-1]

PEER_PAIRS = tuple((i, j) for i in range(PEER_TOPK) for j in range(PEER_TOPK) if (i + 1) * (j + 1) <= PEER_TOPK)
PEER_PAIR_ROWS = -(-len(PEER_PAIRS) // 8) * 8


def _cparams(sem):
    return pltpu.CompilerParams(dimension_semantics=sem, vmem_limit_bytes=VMEM_LIMIT)


def _dot_nt(a, b):
    return lax.dot_general(a, b, (((1,), (1,)), ((), ())), preferred_element_type=F32)


def _bias_of_rel(rel, tab_ref, head, n_heads):
    n = jnp.abs(rel)
    neg = jnp.full(rel.shape, tab_ref[0 * n_heads + head], F32)
    pos = jnp.full(rel.shape, tab_ref[16 * n_heads + head], F32)
    for k, thr in enumerate(BUCKET_THRESHOLDS, start=1):
        ge = n >= thr
        neg = jnp.where(ge, tab_ref[k * n_heads + head], neg)
        pos = jnp.where(ge, tab_ref[(16 + k) * n_heads + head], pos)
    return jnp.where(rel > 0, pos, neg)


def _ada_kernel(c_ref, w_ref, b_ref, o_ref):
    c = c_ref[...]
    sc = c / (1.0 + jnp.exp(-c))
    o_ref[...] = jnp.dot(sc, w_ref[...], preferred_element_type=F32,
                         precision=lax.Precision.HIGHEST) + b_ref[...]


def ada_mod(c, w_ada, b_ada):
    B = c.shape[0]
    n = w_ada.shape[1] // D_MODEL
    return pl.pallas_call(
        _ada_kernel,
        grid=(n,),
        in_specs=[pl.BlockSpec((B, D_MODEL), lambda j: (0, 0)),
                  pl.BlockSpec((D_MODEL, D_MODEL), lambda j: (0, j)),
                  pl.BlockSpec((1, D_MODEL), lambda j: (0, j))],
        out_specs=pl.BlockSpec((B, D_MODEL), lambda j: (0, j)),
        out_shape=jax.ShapeDtypeStruct((B, n * D_MODEL), F32),
        compiler_params=_cparams(("arbitrary",)),
        name="ada_mod",
    )(c, w_ada, b_ada.reshape(1, -1))


def _modulated_norm(x, g, scale, shift):
    ms = jnp.mean(x * x, axis=-1, keepdims=True)
    return (x * lax.rsqrt(ms + EPS) * g) * (1.0 + scale) + shift


def _in_proj_kernel(x_ref, mod_ref, g_ref, w_ref, o_ref, h_ref):
    @pl.when(pl.program_id(2) == 0)
    def _():
        h = _modulated_norm(x_ref[0], g_ref[...], mod_ref[0, 1:2, :], mod_ref[0, 0:1, :])
        h_ref[...] = h.astype(BF16)

    o_ref[0] = jnp.dot(h_ref[...], w_ref[...], preferred_element_type=F32).astype(BF16)


def in_proj(x, mod, g_norm, w_in_bf16, tm=1024, tn=IN_COLS // 4):
    B, S, _ = x.shape
    tm = min(tm, S)
    return pl.pallas_call(
        _in_proj_kernel,
        grid=(B, S // tm, IN_COLS // tn),
        in_specs=[pl.BlockSpec((1, tm, D_MODEL), lambda b, i, j: (b, i, 0)),
                  pl.BlockSpec((1, 6, D_MODEL), lambda b, i, j: (b, 0, 0)),
                  pl.BlockSpec((1, D_MODEL), lambda b, i, j: (0, 0)),
                  pl.BlockSpec((D_MODEL, tn), lambda b, i, j: (0, j))],
        out_specs=pl.BlockSpec((1, tm, tn), lambda b, i, j: (b, i, j)),
        out_shape=jax.ShapeDtypeStruct((B, S, IN_COLS), BF16),
        scratch_shapes=[pltpu.VMEM((tm, D_MODEL), BF16)],
        compiler_params=_cparams(("arbitrary", "arbitrary", "arbitrary")),
        name="in_proj",
    )(x, mod, g_norm.reshape(1, -1), w_in_bf16)


DA_TILE = 512
DA_NBIAS = 2 * (-(-(BUCKET_CONST_FROM + DA_TILE - 1) // DA_TILE)) + 1


def _da_kernel(tab_ref, q_ref, k_ref, v_ref, lq1_ref, lk1_ref, lq2_ref, lk2_ref, gsub_ref, o_ref,
               bias_ref, vt_ref, acc1_ref, acc2_ref, *, n_chunks):
    T = DA_TILE
    half = DA_NBIAS // 2
    head = pl.program_id(0)
    qi = pl.program_id(2)

    @pl.when((pl.program_id(1) == 0) & (qi == 0))
    def _():
        row = lax.broadcasted_iota(jnp.int32, (T, T), 0)
        col = lax.broadcasted_iota(jnp.int32, (T, T), 1)
        for d in range(DA_NBIAS):
            bias_ref[d] = _bias_of_rel((d - half) * T + row - col, tab_ref, head, DA_HEADS)

    @pl.when(qi == 0)
    def _():
        for c in range(n_chunks):
            vt_ref[c] = v_ref[0, c * T:(c + 1) * T, :].astype(F32).T.astype(BF16)

    q = q_ref[0].astype(F32) * (DA_HEAD_DIM ** -0.5)
    lane = lax.broadcasted_iota(jnp.int32, q.shape, 1)
    q1 = jnp.where(lane < DA_HEAD_DIM, q, 0.0).astype(BF16)
    q2 = jnp.where(lane >= DA_HEAD_DIM, q, 0.0).astype(BF16)
    acc1_ref[...] = jnp.zeros_like(acc1_ref)
    acc2_ref[...] = jnp.zeros_like(acc2_ref)

    def one_map(s, m, l, acc_ref, vt):
        m_new = jnp.maximum(m, jnp.max(s, axis=0, keepdims=True))
        alpha = jnp.exp(m - m_new)
        p = jnp.exp(s - m_new)
        l_new = alpha * l + jnp.sum(p, axis=0, keepdims=True)
        acc_ref[...] = alpha * acc_ref[...] + jnp.dot(vt, p.astype(BF16), preferred_element_type=F32)
        return m_new, l_new

    def chunk(kc, carry):
        m1, l1, m2, l2 = carry
        k = k_ref[0, pl.ds(pl.multiple_of(kc * T, T), T), :]
        vt = vt_ref[kc]
        bias = bias_ref[jnp.clip(kc - qi, -half, half) + half]
        m1, l1 = one_map(_dot_nt(k, q1) + bias, m1, l1, acc1_ref, vt)
        m2, l2 = one_map(_dot_nt(k, q2) + bias, m2, l2, acc2_ref, vt)
        return m1, l1, m2, l2

    neg = jnp.full((1, T), -jnp.inf, F32)
    zero = jnp.zeros((1, T), F32)
    m1, l1, m2, l2 = lax.fori_loop(0, n_chunks, chunk, (neg, zero, neg, zero))

    lam = (jnp.exp(jnp.sum(lq1_ref[...] * lk1_ref[...], axis=-1, keepdims=True))
           - jnp.exp(jnp.sum(lq2_ref[...] * lk2_ref[...], axis=-1, keepdims=True)) + LAM_INIT)
    ot = acc1_ref[...] / l1 - lam * (acc2_ref[...] / l2)
    ms = jnp.mean(ot * ot, axis=0, keepdims=True)
    y = ot * lax.rsqrt(ms + EPS) * gsub_ref[...] * (1.0 - LAM_INIT)
    o_ref[0] = y.T.astype(BF16)


def diff_attn(p, tab, lq1, lk1, lq2, lk2, g_subln):
    B, S, _ = p.shape
    T = DA_TILE
    n_chunks = S // T
    small = pl.BlockSpec((1, DA_HEAD_DIM), lambda h, b, i: (0, 0))
    return pl.pallas_call(
        functools.partial(_da_kernel, n_chunks=n_chunks),
        grid=(DA_HEADS, B, S // T),
        in_specs=[pl.BlockSpec(memory_space=pltpu.SMEM),
                  pl.BlockSpec((1, T, LANES), lambda h, b, i: (b, i, OFF_QA // LANES + h)),
                  pl.BlockSpec((1, S, LANES), lambda h, b, i: (b, 0, OFF_KA // LANES + h)),
                  pl.BlockSpec((1, S, LANES), lambda h, b, i: (b, 0, OFF_VA // LANES + h)),
                  small, small, small, small,
                  pl.BlockSpec((2 * DA_HEAD_DIM, 1), lambda h, b, i: (0, 0))],
        out_specs=pl.BlockSpec((1, T, LANES), lambda h, b, i: (b, i, h)),
        out_shape=jax.ShapeDtypeStruct((B, S, DA_V), BF16),
        scratch_shapes=[pltpu.VMEM((DA_NBIAS, T, T), F32),
                        pltpu.VMEM((n_chunks, LANES, T), BF16),
                        pltpu.VMEM((LANES, T), F32),
                        pltpu.VMEM((LANES, T), F32)],
        compiler_params=_cparams(("arbitrary", "arbitrary", "arbitrary")),
        name="diff_attn",
    )(tab, p, p, p, lq1.reshape(1, -1), lk1.reshape(1, -1), lq2.reshape(1, -1), lk2.reshape(1, -1),
      g_subln.reshape(-1, 1))


DIL_TQ = 128
DIL_WIN = 3 * DIL_TQ


def _dil_kernel(tab_ref, q_ref, k_ref, v_ref, o_ref, lse_ref, kpad_ref, vpad_ref, mask_ref, *, L, LQ, dilation):
    TQ, W, hs = DIL_TQ, DIL_WIN, DIL_HALF_SPAN
    n_tiles = L // TQ
    qblock = pl.program_id(2)

    @pl.when((pl.program_id(0) == 0) & (pl.program_id(1) == 0) & (qblock == 0))
    def _():
        row = lax.broadcasted_iota(jnp.int32, (TQ, W), 0)
        col = lax.broadcasted_iota(jnp.int32, (TQ, W), 1)
        rel = col - TQ - row
        band = jnp.abs(rel) <= hs
        for variant, qs in enumerate((0, TQ, L - TQ)):
            kpos = qs - TQ + col
            ok = band & (kpos >= 0) & (kpos < L)
            for h in range(DIL_HEADS):
                mask_ref[variant, h] = jnp.where(ok, _bias_of_rel(dilation * rel, tab_ref, h, DIL_HEADS), NEG_BIG)

    @pl.when(qblock == 0)
    def _():
        pad = jnp.zeros((TQ, DIL_COLS), BF16)
        for ref, src in ((kpad_ref, k_ref), (vpad_ref, v_ref)):
            ref[0:TQ, :] = pad
            ref[TQ + L:2 * TQ + L, :] = pad
            ref[TQ:TQ + L, :] = src[0]

    lane = lax.broadcasted_iota(jnp.int32, (TQ, LANES), 1)
    low = lane < DIL_HEAD_DIM

    def tile(t, _):
        qs = pl.multiple_of(t * TQ, TQ)
        tg = qblock * (LQ // TQ) + t
        ks = pl.multiple_of(tg * TQ, TQ)
        variant = jnp.where(tg == 0, 0, jnp.where(tg == n_tiles - 1, 2, 1))
        lse_tile = jnp.zeros((TQ, LANES), F32)
        for hp in range(DIL_HEADS // 2):
            cols = slice(hp * LANES, (hp + 1) * LANES)
            q = q_ref[0, pl.ds(qs, TQ), cols].astype(F32) * (DIL_HEAD_DIM ** -0.5)
            kw = kpad_ref[pl.ds(ks, W), cols]
            vw = vpad_ref[pl.ds(ks, W), cols]
            outs = []
            for e in range(2):
                h = 2 * hp + e
                qm = jnp.where(low if e == 0 else ~low, q, 0.0).astype(BF16)
                s = _dot_nt(qm, kw) + mask_ref[variant, h]
                m = jnp.max(s, axis=-1, keepdims=True)
                pr = jnp.exp(s - m)
                den = jnp.sum(pr, axis=-1, keepdims=True)
                outs.append(jnp.dot(pr.astype(BF16), vw, preferred_element_type=F32) / den)
                lse_tile = jnp.where(lane == h, m + jnp.log(den), lse_tile)
            o_ref[0, pl.ds(qs, TQ), cols] = jnp.where(low, outs[0], outs[1])
        lse_ref[0, pl.ds(qs, TQ), :] = lse_tile
        return 0

    lax.fori_loop(0, LQ // TQ, tile, 0)


def dilated_group(p, tab, group, dilation, lq=1024):
    B, S, _ = p.shape
    r = dilation
    L = S // r
    assert L >= 2 * DIL_TQ and L % DIL_TQ == 0
    LQ = min(lq, L)
    blocks_per_row = IN_COLS // DIL_COLS
    pv = p.reshape(B, L, r * IN_COLS)

    def col(off, rows, tiled):
        cb = off // DIL_COLS + group
        return pl.BlockSpec((1, rows, DIL_COLS),
                            lambda b, rho, t: (b, t if tiled else 0, rho * blocks_per_row + cb))

    o, lse = pl.pallas_call(
        functools.partial(_dil_kernel, L=L, LQ=LQ, dilation=r),
        grid=(B, r, L // LQ),
        in_specs=[pl.BlockSpec(memory_space=pltpu.SMEM),
                  col(OFF_QB, LQ, True), col(OFF_KB, L, False), col(OFF_VB, L, False)],
        out_specs=[pl.BlockSpec((1, LQ, DIL_COLS), lambda b, rho, t: (b, t, rho)),
                   pl.BlockSpec((1, LQ, LANES), lambda b, rho, t: (b, t, rho))],
        out_shape=[jax.ShapeDtypeStruct((B, L, r * DIL_COLS), F32),
                   jax.ShapeDtypeStruct((B, L, r * LANES), F32)],
        scratch_shapes=[pltpu.VMEM((L + 2 * DIL_TQ, DIL_COLS), BF16),
                        pltpu.VMEM((L + 2 * DIL_TQ, DIL_COLS), BF16),
                        pltpu.VMEM((3, DIL_HEADS, DIL_TQ, DIL_WIN), F32)],
        compiler_params=_cparams(("arbitrary", "arbitrary", "arbitrary")),
        name=f"dilated_g{group}",
    )(tab, pv, pv, pv)
    return o.reshape(B, S, DIL_COLS), lse.reshape(B, S, LANES)


def _split_dot(w, e_ref):
    hi = w.astype(BF16)
    lo = (w - hi.astype(F32)).astype(BF16)
    e = e_ref[...]
    return jnp.dot(hi, e, preferred_element_type=F32) + jnp.dot(lo, e, preferred_element_type=F32)


def _merge_kernel(x_ref, mod_ref, oa_ref, o0_ref, o1_ref, o2_ref, l0_ref, l1_ref, l2_ref,
                  ga0_ref, ga1_ref, gb0_ref, gb1_ref, e_ref, wpa_ref, wpb_ref, wout_ref, gffn_ref,
                  x1_ref, h2_ref):
    a = jnp.dot(oa_ref[0], wpa_ref[...], preferred_element_type=F32)
    l0, l1, l2 = l0_ref[0], l1_ref[0], l2_ref[0]
    mx = jnp.maximum(jnp.maximum(l0, l1), l2)
    e0, e1, e2 = jnp.exp(l0 - mx), jnp.exp(l1 - mx), jnp.exp(l2 - mx)
    den = e0 + e1 + e2
    ob = (o0_ref[0] * _split_dot(e0 / den, e_ref) + o1_ref[0] * _split_dot(e1 / den, e_ref)
          + o2_ref[0] * _split_dot(e2 / den, e_ref))
    b = jnp.dot(ob.astype(BF16), wpb_ref[...], preferred_element_type=F32)
    ga = jnp.concatenate([ga0_ref[0], ga1_ref[0]], axis=-1).astype(F32)
    gb = jnp.concatenate([gb0_ref[0], gb1_ref[0]], axis=-1).astype(F32)
    merged = a / (1.0 + jnp.exp(-ga)) + b / (1.0 + jnp.exp(-gb))
    mix = jnp.dot(merged.astype(BF16), wout_ref[...], preferred_element_type=F32)
    x1 = x_ref[0] + mod_ref[0, 2:3, :] * mix
    x1_ref[0] = x1
    h2_ref[0] = _modulated_norm(x1, gffn_ref[...], mod_ref[0, 4:5, :], mod_ref[0, 3:4, :]).astype(BF16)


def merge_proj(x, mod, p, oa, outs, lses, wpa, wpb, wout, g_ffn, tm=512):
    B, S, _ = x.shape
    tm = min(tm, S)
    expand = np.zeros((LANES, DIL_COLS), np.float32)
    for h in range(DIL_HEADS):
        expand[h, h * DIL_HEAD_DIM:(h + 1) * DIL_HEAD_DIM] = 1.0
    expand = jnp.asarray(expand, BF16)

    def tok(w):
        return pl.BlockSpec((1, tm, w), lambda b, i: (b, i, 0))

    def pcol(off):
        return pl.BlockSpec((1, tm, DIL_COLS), lambda b, i: (b, i, off // DIL_COLS))

    def whole(a):
        return pl.BlockSpec(a.shape, lambda b, i: (0,) * a.ndim)

    g = g_ffn.reshape(1, -1)
    return pl.pallas_call(
        _merge_kernel,
        grid=(B, S // tm),
        in_specs=[tok(D_MODEL), pl.BlockSpec((1, 6, D_MODEL), lambda b, i: (b, 0, 0)), tok(DA_V),
                  tok(DIL_COLS), tok(DIL_COLS), tok(DIL_COLS), tok(LANES), tok(LANES), tok(LANES),
                  pcol(OFF_GA), pcol(OFF_GA + DIL_COLS), pcol(OFF_GB), pcol(OFF_GB + DIL_COLS),
                  whole(expand), whole(wpa), whole(wpb), whole(wout), whole(g)],
        out_specs=[tok(D_MODEL), tok(D_MODEL)],
        out_shape=[jax.ShapeDtypeStruct((B, S, D_MODEL), F32), jax.ShapeDtypeStruct((B, S, D_MODEL), BF16)],
        compiler_params=_cparams(("arbitrary", "arbitrary")),
        name="merge_proj",
    )(x, mod, oa, outs[0], outs[1], outs[2], lses[0], lses[1], lses[2], p, p, p, p, expand, wpa, wpb, wout, g)


def _top_values(s, count):
    vals = []
    for _ in range(count):
        m = jnp.max(s, axis=0, keepdims=True)
        vals.append(m)
        s = jnp.where(s >= m, -jnp.inf, s)
    return vals


def _route_kernel(h_ref, wq_ref, k1_ref, k2_ref, s1_ref, s2_ref, e1_ref, e2_ref, tau_ref):
    qp = jnp.dot(h_ref[...], wq_ref[...], preferred_element_type=F32)
    tm = qp.shape[0]
    row = lax.broadcasted_iota(jnp.int32, (PEER_PAIR_ROWS, tm), 0)
    head_row = lax.broadcasted_iota(jnp.int32, (PEER_HEADS, tm), 0)
    tau_all = jnp.zeros((PEER_HEADS, tm), F32)
    for h in range(PEER_HEADS):
        q1 = qp[:, h * PEER_QUERY_DIM:h * PEER_QUERY_DIM + PEER_HALF].astype(BF16)
        q2 = qp[:, h * PEER_QUERY_DIM + PEER_HALF:(h + 1) * PEER_QUERY_DIM].astype(BF16)
        s1 = _dot_nt(k1_ref[h], q1)
        s2 = _dot_nt(k2_ref[h], q2)
        v1 = _top_values(s1, PEER_TOPK)
        v2 = _top_values(s2, PEER_TOPK)
        cand = jnp.full((PEER_PAIR_ROWS, tm), -jnp.inf, F32)
        for r, (i, j) in enumerate(PEER_PAIRS):
            cand = jnp.where(row == r, v1[i] + v2[j], cand)
        tau = _top_values(cand, PEER_TOPK)[-1]
        top = v1[0] + v2[0]
        z = jnp.sum(jnp.where(cand >= tau, jnp.exp(cand - top), 0.0), axis=0, keepdims=True)
        s1_ref[h] = s1
        s2_ref[h] = s2
        e1_ref[h] = jnp.exp(s1 - v1[0])
        e2_ref[h] = jnp.exp(s2 - v2[0]) / z
        tau_all = jnp.where(head_row == h, tau, tau_all)
    tau_ref[...] = tau_all


def peer_route(h2, wq, k1, k2, tm=512):
    T = h2.shape[0]
    tm = min(tm, T)
    big = jax.ShapeDtypeStruct((PEER_HEADS, N_KEYS, T), F32)
    bspec = pl.BlockSpec((PEER_HEADS, N_KEYS, tm), lambda i: (0, 0, i))
    kspec = pl.BlockSpec((PEER_HEADS, N_KEYS, PEER_HALF), lambda i: (0, 0, 0))
    return pl.pallas_call(
        _route_kernel,
        grid=(T // tm,),
        in_specs=[pl.BlockSpec((tm, D_MODEL), lambda i: (i, 0)),
                  pl.BlockSpec(wq.shape, lambda i: (0, 0)), kspec, kspec],
        out_specs=[bspec, bspec, bspec, bspec, pl.BlockSpec((PEER_HEADS, tm), lambda i: (0, i))],
        out_shape=[big, big, big, big, jax.ShapeDtypeStruct((PEER_HEADS, T), F32)],
        compiler_params=_cparams(("arbitrary",)),
        name="peer_route",
    )(h2, wq, k1, k2)


PEER_TE = 1024


def _erf_gelu(z):
    return 0.5 * z * (1.0 + lax.erf(z * (2.0 ** -0.5)))


def _dense_kernel(h_ref, u_ref, vt_ref, s1_ref, s2_ref, e1_ref, e2_ref, tau_ref, x1_ref, mod_ref, gfin_ref,
                  y_ref, acc_ref, a_ref):
    j = pl.program_id(1)

    @pl.when(j == 0)
    def _():
        acc_ref[...] = jnp.zeros_like(acc_ref)

    zt = _dot_nt(u_ref[...], h_ref[...])
    for al in range(PEER_TE // N_KEYS):
        a = j * (PEER_TE // N_KEYS) + al
        w = jnp.zeros((N_KEYS, zt.shape[1]), F32)
        for h in range(PEER_HEADS):
            joint = s2_ref[h] + s1_ref[h, pl.ds(a, 1), :]
            gate = e2_ref[h] * e1_ref[h, pl.ds(a, 1), :]
            w = w + jnp.where(joint >= tau_ref[h:h + 1, :], gate, 0.0)
        rows = slice(al * N_KEYS, (al + 1) * N_KEYS)
        a_ref[rows, :] = (_erf_gelu(zt[rows, :]) * w).astype(BF16)
    acc_ref[...] += jnp.dot(vt_ref[...], a_ref[...], preferred_element_type=F32)

    @pl.when(j == pl.num_programs(1) - 1)
    def _():
        x2 = x1_ref[...] + mod_ref[0, 5:6, :] * acc_ref[...].T
        ms = jnp.mean(x2 * x2, axis=-1, keepdims=True)
        y_ref[...] = x2 * lax.rsqrt(ms + EPS) * gfin_ref[...]


def peer_dense(h2, u_bf16, vt_bf16, routes, x1, mod, g_final, S, tm=512):
    T = h2.shape[0]
    tm = min(tm, S)
    s1, s2, e1, e2, tau = routes
    rspec = pl.BlockSpec((PEER_HEADS, N_KEYS, tm), lambda i, j: (0, 0, i))
    tiles_per_row = S // tm
    return pl.pallas_call(
        _dense_kernel,
        grid=(T // tm, N_EXPERTS // PEER_TE),
        in_specs=[pl.BlockSpec((tm, D_MODEL), lambda i, j: (i, 0)),
                  pl.BlockSpec((PEER_TE, D_MODEL), lambda i, j: (j, 0)),
                  pl.BlockSpec((D_MODEL, PEER_TE), lambda i, j: (0, j)),
                  rspec, rspec, rspec, rspec,
                  pl.BlockSpec((PEER_HEADS, tm), lambda i, j: (0, i)),
                  pl.BlockSpec((tm, D_MODEL), lambda i, j: (i, 0)),
                  pl.BlockSpec((1, 6, D_MODEL), lambda i, j: (i // tiles_per_row, 0, 0)),
                  pl.BlockSpec((1, D_MODEL), lambda i, j: (0, 0))],
        out_specs=pl.BlockSpec((tm, D_MODEL), lambda i, j: (i, 0)),
        out_shape=jax.ShapeDtypeStruct((T, D_MODEL), F32),
        scratch_shapes=[pltpu.VMEM((D_MODEL, tm), F32), pltpu.VMEM((PEER_TE, tm), BF16)],
        compiler_params=_cparams(("arbitrary", "arbitrary")),
        name="peer_dense",
    )(h2, u_bf16, vt_bf16, s1, s2, e1, e2, tau, x1, mod, g_final.reshape(1, -1))


def _prepare_weights(w_in, w_proj_a, w_proj_b, w_out, rel_bias, w_query, sub_keys_1, sub_keys_2,
                     expert_u, expert_v):
    tabs = [rel_bias[:, :DA_HEADS].reshape(-1)]
    for gi in range(len(DIL_GROUPS)):
        lo = DA_HEADS + gi * DIL_HEADS
        tabs.append(rel_bias[:, lo:lo + DIL_HEADS].reshape(-1))
    return dict(
        w_in=w_in[0].astype(BF16), wpa=w_proj_a[0].astype(BF16), wpb=w_proj_b[0].astype(BF16),
        wout=w_out[0].astype(BF16), tabs=tabs, wq=w_query[0].astype(BF16),
        k1=sub_keys_1[0].astype(BF16), k2=sub_keys_2[0].astype(BF16),
        u=expert_u[0].astype(BF16), vt=expert_v[0].astype(BF16).T)


def _encoder(x, c, w_ada, b_ada, g_norm_mix, g_norm_ffn, lq1, lk1, lq2, lk2, g_subln, g_final, wts):
    B, S, _ = x.shape
    mod = ada_mod(c, w_ada[0], b_ada[0]).reshape(B, 6, D_MODEL)
    p = in_proj(x, mod, g_norm_mix[0], wts["w_in"])
    oa = diff_attn(p, wts["tabs"][0], lq1[0], lk1[0], lq2[0], lk2[0], g_subln[0])
    outs, lses = [], []
    for gi, (_, dilation) in enumerate(DIL_GROUPS):
        o_g, lse_g = dilated_group(p, wts["tabs"][1 + gi], gi, dilation)
        outs.append(o_g)
        lses.append(lse_g)
    x1, h2 = merge_proj(x, mod, p, oa, outs, lses, wts["wpa"], wts["wpb"], wts["wout"], g_norm_ffn[0])
    h2 = h2.reshape(B * S, D_MODEL)
    routes = peer_route(h2, wts["wq"], wts["k1"], wts["k2"])
    y = peer_dense(h2, wts["u"], wts["vt"], routes, x1.reshape(B * S, D_MODEL), mod, g_final, S)
    return y.reshape(B, S, D_MODEL)


def kernel(x_prompt, x_sample, c_prompt, c_sample, w_ada, b_ada, g_norm_mix, g_norm_ffn, w_in, w_proj_a, w_proj_b, w_out, lambda_q1, lambda_k1, lambda_q2, lambda_k2, g_subln, rel_bias, w_query, sub_keys_1, sub_keys_2, expert_u, expert_v, g_final):
    wts = _prepare_weights(w_in, w_proj_a, w_proj_b, w_out, rel_bias, w_query, sub_keys_1, sub_keys_2,
                           expert_u, expert_v)
    args = (w_ada, b_ada, g_norm_mix, g_norm_ffn, lambda_q1, lambda_k1, lambda_q2, lambda_k2, g_subln, g_final, wts)
    return _encoder(x_prompt, c_prompt, *args), _encoder(x_sample, c_sample, *args)
```

```python
import functools
import math

import jax
import jax.numpy as jnp
import numpy as np
from jax import lax
from jax.experimental import pallas as pl
from jax.experimental.pallas import tpu as pltpu

F32 = jnp.float32
BF16 = jnp.bfloat16

D_MODEL = 1024
DA_HEADS = 8
DA_HEAD_DIM = 64
DA_QK = DA_HEADS * 2 * DA_HEAD_DIM
DA_V = DA_HEADS * 2 * DA_HEAD_DIM
DIL_GROUPS = ((128, 1), (512, 4), (2048, 16))
DIL_HEADS = 8
DIL_HEAD_DIM = 64
DIL_HALF_SPAN = 64
DIL_COLS = DIL_HEADS * DIL_HEAD_DIM
DIL_QKV = len(DIL_GROUPS) * DIL_COLS
IN_COLS = 2 * DA_QK + DA_V + 3 * DIL_QKV + 2 * D_MODEL
NUM_BUCKETS = 32
N_REL_HEADS = DA_HEADS + len(DIL_GROUPS) * DIL_HEADS
PEER_HEADS = 8
N_KEYS = 128
N_EXPERTS = N_KEYS * N_KEYS
PEER_QUERY_DIM = 256
PEER_HALF = PEER_QUERY_DIM // 2
PEER_TOPK = 16
EPS = 1e-6
NEG_BIG = -1e30
LAM_INIT = 0.8 - 0.6 * math.exp(-0.3 * 0)

LANES = 128
VMEM_LIMIT = 56 * 1024 * 1024

OFF_QA, OFF_KA, OFF_VA = 0, DA_QK, 2 * DA_QK
OFF_QB = 2 * DA_QK + DA_V
OFF_KB = OFF_QB + DIL_QKV
OFF_VB = OFF_KB + DIL_QKV
OFF_GA = OFF_VB + DIL_QKV
OFF_GB = OFF_GA + D_MODEL
NAT_Q0 = OFF_QB
NAT_GA = NAT_Q0 + 3 * DIL_COLS
NAT_GB = NAT_GA + D_MODEL
NAT_COLS = NAT_GB + D_MODEL

BUCKET_THRESHOLDS = (1, 2, 3, 4, 5, 6, 7, 8, 15, 27, 50, 91, 166, 305, 559)
BUCKET_CONST_FROM = BUCKET_THRESHOLDS[-1]

PEER_PAIRS = tuple((i, j) for i in range(PEER_TOPK) for j in range(PEER_TOPK) if (i + 1) * (j + 1) <= PEER_TOPK)
PEER_PAIR_ROWS = -(-len(PEER_PAIRS) // 8) * 8


def _cparams(sem):
    return pltpu.CompilerParams(dimension_semantics=sem, vmem_limit_bytes=VMEM_LIMIT)


def _dot_nt(a, b):
    return lax.dot_general(a, b, (((1,), (1,)), ((), ())), preferred_element_type=F32)


def _bias_of_rel(rel, tab_ref, head, n_heads):
    n = jnp.abs(rel)
    neg = jnp.full(rel.shape, tab_ref[0 * n_heads + head], F32)
    pos = jnp.full(rel.shape, tab_ref[16 * n_heads + head], F32)
    for k, thr in enumerate(BUCKET_THRESHOLDS, start=1):
        ge = n >= thr
        neg = jnp.where(ge, tab_ref[k * n_heads + head], neg)
        pos = jnp.where(ge, tab_ref[(16 + k) * n_heads + head], pos)
    return jnp.where(rel > 0, pos, neg)


def _ada_kernel(c_ref, w_ref, b_ref, o_ref):
    c = c_ref[...]
    sc = c / (1.0 + jnp.exp(-c))
    o_ref[...] = jnp.dot(sc, w_ref[...], preferred_element_type=F32,
                         precision=lax.Precision.HIGHEST) + b_ref[...]


def ada_mod(c, w_ada, b_ada):
    B = c.shape[0]
    n = w_ada.shape[1] // D_MODEL
    return pl.pallas_call(
        _ada_kernel,
        grid=(n,),
        in_specs=[pl.BlockSpec((B, D_MODEL), lambda j: (0, 0)),
                  pl.BlockSpec((D_MODEL, D_MODEL), lambda j: (0, j)),
                  pl.BlockSpec((1, D_MODEL), lambda j: (0, j))],
        out_specs=pl.BlockSpec((B, D_MODEL), lambda j: (0, j)),
        out_shape=jax.ShapeDtypeStruct((B, n * D_MODEL), F32),
        compiler_params=_cparams(("arbitrary",)),
        name="ada_mod",
    )(c, w_ada, b_ada.reshape(1, -1))


def _modulated_norm(x, g, scale, shift):
    ms = jnp.mean(x * x, axis=-1, keepdims=True)
    return (x * lax.rsqrt(ms + EPS) * g) * (1.0 + scale) + shift


def _in_proj_kernel(x_ref, mod_ref, g_ref, w_ref, o_ref, h_ref, *hs_ref, r):
    tm = h_ref.shape[0]
    n = tm // r

    @pl.when(pl.program_id(2) == 0)
    def _():
        h = _modulated_norm(x_ref[0], g_ref[...], mod_ref[0, 1:2, :], mod_ref[0, 0:1, :])
        if r == 1:
            h_ref[...] = h.astype(BF16)
        else:
            for c in range(D_MODEL // LANES):
                hs_ref[0][c] = h[:, c * LANES:(c + 1) * LANES]
            for c in range(D_MODEL // LANES):
                h_ref[:, c * LANES:(c + 1) * LANES] = jnp.concatenate(
                    [hs_ref[0][c, pl.ds(rho, n, stride=r), :] for rho in range(r)], axis=0).astype(BF16)

    res = jnp.dot(h_ref[...], w_ref[...], preferred_element_type=F32).astype(BF16)
    if r == 1:
        o_ref[0] = res
    else:
        for rho in range(r):
            o_ref[0, rho] = res[rho * n:(rho + 1) * n, :]


def in_proj(x, mod, g_norm, w_bf16, r=1, tm=512, tn=None):
    B, S, _ = x.shape
    N = w_bf16.shape[1]
    tn = N if tn is None else tn
    tm = min(tm, S)
    if r == 1:
        out_spec = pl.BlockSpec((1, tm, tn), lambda b, i, j: (b, i, j))
        out_shape = jax.ShapeDtypeStruct((B, S, N), BF16)
        scratch = [pltpu.VMEM((tm, D_MODEL), BF16)]
    else:
        out_spec = pl.BlockSpec((1, r, tm // r, tn), lambda b, i, j: (b, 0, i, j))
        out_shape = jax.ShapeDtypeStruct((B, r, S // r, N), BF16)
        scratch = [pltpu.VMEM((tm, D_MODEL), BF16), pltpu.VMEM((D_MODEL // LANES, tm, LANES), F32)]
    return pl.pallas_call(
        functools.partial(_in_proj_kernel, r=r),
        grid=(B, S // tm, N // tn),
        in_specs=[pl.BlockSpec((1, tm, D_MODEL), lambda b, i, j: (b, i, 0)),
                  pl.BlockSpec((1, 6, D_MODEL), lambda b, i, j: (b, 0, 0)),
                  pl.BlockSpec((1, D_MODEL), lambda b, i, j: (0, 0)),
                  pl.BlockSpec((D_MODEL, tn), lambda b, i, j: (0, j))],
        out_specs=out_spec,
        out_shape=out_shape,
        scratch_shapes=scratch,
        compiler_params=_cparams(("arbitrary", "arbitrary", "arbitrary")),
        name=f"in_proj_r{r}",
    )(x, mod, g_norm.reshape(1, -1), w_bf16)


DA_TQ = 256
DA_TK = 512
DA_D_LO = -(-(BUCKET_CONST_FROM - 1 + DA_TK) // DA_TQ)
DA_D_HI = -(-(BUCKET_CONST_FROM - 1 + DA_TQ) // DA_TQ)
DA_NBIAS = DA_D_LO + DA_D_HI + 1
DA_VALUE_PARTS = 2
DA_VROWS = 144
LOG2E = 1.4426950408889634


def _da_kernel(tab_ref, q_ref, k_ref, v_ref, lq1_ref, lk1_ref, lq2_ref, lk2_ref, gsub_ref, o_ref,
               bias_ref, vt_ref, s_ref, p_ref, *, n_chunks):
    TQ, TK = DA_TQ, DA_TK
    head = pl.program_id(0)
    qi = pl.program_id(2)

    @pl.when((pl.program_id(1) == 0) & (qi == 0))
    def _():
        row = lax.broadcasted_iota(jnp.int32, (TK, TQ), 0)
        col = lax.broadcasted_iota(jnp.int32, (TK, TQ), 1)
        for d in range(DA_NBIAS):
            bias_ref[d] = _bias_of_rel((d - DA_D_LO) * TQ + row - col, tab_ref, head, DA_HEADS) * LOG2E

    @pl.when(qi == 0)
    def _():
        extra = lax.broadcasted_iota(jnp.int32, (DA_VROWS - LANES, TK), 0)
        ones_row = jnp.where(extra == 0, 1.0, 0.0).astype(BF16)
        for c in range(n_chunks):
            vt_ref[0:LANES, c * TK:(c + 1) * TK] = v_ref[0, c * TK:(c + 1) * TK, :].astype(F32).T.astype(BF16)
            vt_ref[LANES:DA_VROWS, c * TK:(c + 1) * TK] = ones_row

    q = q_ref[0].astype(F32) * (DA_HEAD_DIM ** -0.5 * LOG2E)
    lane = lax.broadcasted_iota(jnp.int32, q.shape, 1)
    q1 = jnp.where(lane < DA_HEAD_DIM, q, 0.0).astype(BF16)
    q2 = jnp.where(lane >= DA_HEAD_DIM, q, 0.0).astype(BF16)

    def score_chunk(kc, carry):
        m1, m2 = carry
        k = k_ref[0, pl.ds(pl.multiple_of(kc * TK, TK), TK), :]
        d = kc * (TK // TQ) - qi
        bias = bias_ref[jnp.clip(d, -DA_D_LO, DA_D_HI) + DA_D_LO]
        s1 = _dot_nt(k, q1) + bias
        s2 = _dot_nt(k, q2) + bias
        s_ref[0, kc] = s1
        s_ref[1, kc] = s2
        return (jnp.maximum(m1, jnp.max(s1, axis=0, keepdims=True)),
                jnp.maximum(m2, jnp.max(s2, axis=0, keepdims=True)))

    neg = jnp.full((1, TQ), -jnp.inf, F32)
    m1, m2 = lax.fori_loop(0, n_chunks, score_chunk, (neg, neg), unroll=4)

    per_part = n_chunks // DA_VALUE_PARTS
    a1 = a2 = None
    for part in range(DA_VALUE_PARTS):
        for c in range(part * per_part, (part + 1) * per_part):
            p_ref[0, c * TK:(c + 1) * TK, :] = jnp.exp2((s_ref[0, c] - m1).astype(BF16))
            p_ref[1, c * TK:(c + 1) * TK, :] = jnp.exp2((s_ref[1, c] - m2).astype(BF16))
        keys = slice(part * per_part * TK, (part + 1) * per_part * TK)
        d1 = jnp.dot(vt_ref[:, keys], p_ref[0, keys, :], preferred_element_type=F32)
        d2 = jnp.dot(vt_ref[:, keys], p_ref[1, keys, :], preferred_element_type=F32)
        a1 = d1 if a1 is None else a1 + d1
        a2 = d2 if a2 is None else a2 + d2

    lam = (jnp.exp(jnp.sum(lq1_ref[...] * lk1_ref[...], axis=-1, keepdims=True))
           - jnp.exp(jnp.sum(lq2_ref[...] * lk2_ref[...], axis=-1, keepdims=True)) + LAM_INIT)
    ot = a1[0:LANES] / a1[LANES:LANES + 1] - lam * (a2[0:LANES] / a2[LANES:LANES + 1])
    ms = jnp.mean(ot * ot, axis=0, keepdims=True)
    y = ot * lax.rsqrt(ms + EPS) * gsub_ref[...] * (1.0 - LAM_INIT)
    o_ref[0] = y.T.astype(BF16)


def diff_attn(p, tab, lq1, lk1, lq2, lk2, g_subln):
    B, S, _ = p.shape
    TQ, TK = DA_TQ, DA_TK
    n_chunks = S // TK
    small = pl.BlockSpec((1, DA_HEAD_DIM), lambda h, b, i: (0, 0))
    return pl.pallas_call(
        functools.partial(_da_kernel, n_chunks=n_chunks),
        grid=(DA_HEADS, B, S // TQ),
        in_specs=[pl.BlockSpec(memory_space=pltpu.SMEM),
                  pl.BlockSpec((1, TQ, LANES), lambda h, b, i: (b, i, OFF_QA // LANES + h)),
                  pl.BlockSpec((1, S, LANES), lambda h, b, i: (b, 0, OFF_KA // LANES + h)),
                  pl.BlockSpec((1, S, LANES), lambda h, b, i: (b, 0, OFF_VA // LANES + h)),
                  small, small, small, small,
                  pl.BlockSpec((2 * DA_HEAD_DIM, 1), lambda h, b, i: (0, 0))],
        out_specs=pl.BlockSpec((1, TQ, LANES), lambda h, b, i: (b, i, h)),
        out_shape=jax.ShapeDtypeStruct((B, S, DA_V), BF16),
        scratch_shapes=[pltpu.VMEM((DA_NBIAS, TK, TQ), F32),
                        pltpu.VMEM((DA_VROWS, S), BF16),
                        pltpu.VMEM((2, n_chunks, TK, TQ), F32),
                        pltpu.VMEM((2, S, TQ), BF16)],
        compiler_params=_cparams(("arbitrary", "arbitrary", "arbitrary")),
        name="diff_attn",
    )(tab, p, p, p, lq1.reshape(1, -1), lk1.reshape(1, -1), lq2.reshape(1, -1), lk2.reshape(1, -1),
      g_subln.reshape(-1, 1))


DIL_TQ = 128
DIL_WIN = 3 * DIL_TQ


def _dil_kernel(tab_ref, q_ref, k_ref, v_ref, o_ref, lse_ref, kpad_ref, vpad_ref, mask_ref, *, L, LQ, dilation):
    TQ, W, hs = DIL_TQ, DIL_WIN, DIL_HALF_SPAN
    n_tiles = L // TQ
    qblock = pl.program_id(2)

    @pl.when((pl.program_id(0) == 0) & (pl.program_id(1) == 0) & (qblock == 0))
    def _():
        row = lax.broadcasted_iota(jnp.int32, (TQ, W), 0)
        col = lax.broadcasted_iota(jnp.int32, (TQ, W), 1)
        rel = col - TQ - row
        band = jnp.abs(rel) <= hs
        for variant, qs in enumerate((0, TQ, L - TQ)):
            kpos = qs - TQ + col
            ok = band & (kpos >= 0) & (kpos < L)
            for h in range(DIL_HEADS):
                mask_ref[variant, h] = jnp.where(ok, _bias_of_rel(dilation * rel, tab_ref, h, DIL_HEADS), NEG_BIG)

    @pl.when(qblock == 0)
    def _():
        pad = jnp.zeros((TQ, DIL_COLS), BF16)
        for ref, src in ((kpad_ref, k_ref), (vpad_ref, v_ref)):
            ref[0:TQ, :] = pad
            ref[TQ + L:2 * TQ + L, :] = pad
            ref[TQ:TQ + L, :] = src[0]

    lane = lax.broadcasted_iota(jnp.int32, (TQ, LANES), 1)
    low = lane < DIL_HEAD_DIM

    def tile(t, _):
        qs = pl.multiple_of(t * TQ, TQ)
        tg = qblock * (LQ // TQ) + t
        ks = pl.multiple_of(tg * TQ, TQ)
        variant = jnp.where(tg == 0, 0, jnp.where(tg == n_tiles - 1, 2, 1))
        lse_tile = jnp.zeros((TQ, LANES), F32)
        scores = []
        for h in range(DIL_HEADS):
            cols = slice(h // 2 * LANES, (h // 2 + 1) * LANES)
            q = q_ref[0, pl.ds(qs, TQ), cols].astype(F32) * (DIL_HEAD_DIM ** -0.5)
            qm = jnp.where(low if h % 2 == 0 else ~low, q, 0.0).astype(BF16)
            scores.append(_dot_nt(qm, kpad_ref[pl.ds(ks, W), cols]) + mask_ref[variant, h])
        probs, dens = [], []
        for h in range(DIL_HEADS):
            m = jnp.max(scores[h], axis=-1, keepdims=True)
            pr = jnp.exp(scores[h] - m)
            den = jnp.sum(pr, axis=-1, keepdims=True)
            probs.append(pr.astype(BF16))
            dens.append(den)
            lse_tile = jnp.where(lane == h, m + jnp.log(den), lse_tile)
        for hp in range(DIL_HEADS // 2):
            cols = slice(hp * LANES, (hp + 1) * LANES)
            vw = vpad_ref[pl.ds(ks, W), cols]
            outs = [jnp.dot(probs[2 * hp + e], vw, preferred_element_type=F32) / dens[2 * hp + e] for e in range(2)]
            o_ref[0, pl.ds(qs, TQ), cols] = jnp.where(low, outs[0], outs[1])
        lse_ref[0, pl.ds(qs, TQ), :] = lse_tile
        return 0

    lax.fori_loop(0, LQ // TQ, tile, 0)


def dilated_group(pg, tab, qkv_col, dilation, lq=1024):
    B, r, L, _ = pg.shape
    assert r == dilation and L >= 2 * DIL_TQ and L % DIL_TQ == 0
    LQ = min(lq, L)

    def col(which, rows, tiled):
        cb = qkv_col // DIL_COLS + which
        return pl.BlockSpec((1, None, rows, DIL_COLS), lambda b, rho, t: (b, rho, t if tiled else 0, cb))

    return pl.pallas_call(
        functools.partial(_dil_kernel, L=L, LQ=LQ, dilation=r),
        grid=(B, r, L // LQ),
        in_specs=[pl.BlockSpec(memory_space=pltpu.SMEM),
                  col(0, LQ, True), col(1, L, False), col(2, L, False)],
        out_specs=[pl.BlockSpec((1, None, LQ, DIL_COLS), lambda b, rho, t: (b, rho, t, 0)),
                   pl.BlockSpec((1, None, LQ, LANES), lambda b, rho, t: (b, rho, t, 0))],
        out_shape=[jax.ShapeDtypeStruct((B, r, L, DIL_COLS), F32),
                   jax.ShapeDtypeStruct((B, r, L, LANES), F32)],
        scratch_shapes=[pltpu.VMEM((L + 2 * DIL_TQ, DIL_COLS), BF16),
                        pltpu.VMEM((L + 2 * DIL_TQ, DIL_COLS), BF16),
                        pltpu.VMEM((3, DIL_HEADS, DIL_TQ, DIL_WIN), F32)],
        compiler_params=_cparams(("arbitrary", "arbitrary", "arbitrary")),
        name=f"dilated_r{r}",
    )(tab, pg, pg, pg)


def _split_dot(w, e_ref):
    hi = w.astype(BF16)
    lo = (w - hi.astype(F32)).astype(BF16)
    e = e_ref[...]
    return jnp.dot(hi, e, preferred_element_type=F32) + jnp.dot(lo, e, preferred_element_type=F32)


def _token_order(src_ref, tmp_ref, r):
    if r == 1:
        return src_ref[0, 0]
    n, width = src_ref.shape[2], src_ref.shape[3]
    for c in range(width // LANES):
        for rho in range(r):
            tmp_ref[c, pl.ds(rho, n, stride=r), :] = src_ref[0, rho, :, c * LANES:(c + 1) * LANES]
    return jnp.concatenate([tmp_ref[c] for c in range(width // LANES)], axis=1)


def _merge_kernel(x_ref, mod_ref, oa_ref, o0_ref, o1_ref, o2_ref, l0_ref, l1_ref, l2_ref,
                  ga0_ref, ga1_ref, gb0_ref, gb1_ref, e_ref, wpa_ref, wpb_ref, wout_ref, gffn_ref,
                  x1_ref, h2_ref, to1_ref, to2_ref, tl1_ref, tl2_ref):
    dil = [d for _, d in DIL_GROUPS]
    a = jnp.dot(oa_ref[0], wpa_ref[...], preferred_element_type=F32)
    l0 = _token_order(l0_ref, None, dil[0])
    l1 = _token_order(l1_ref, tl1_ref, dil[1])
    l2 = _token_order(l2_ref, tl2_ref, dil[2])
    mx = jnp.maximum(jnp.maximum(l0, l1), l2)
    e0, e1, e2 = jnp.exp(l0 - mx), jnp.exp(l1 - mx), jnp.exp(l2 - mx)
    den = e0 + e1 + e2
    ob = (_token_order(o0_ref, None, dil[0]) * _split_dot(e0 / den, e_ref)
          + _token_order(o1_ref, to1_ref, dil[1]) * _split_dot(e1 / den, e_ref)
          + _token_order(o2_ref, to2_ref, dil[2]) * _split_dot(e2 / den, e_ref))
    b = jnp.dot(ob.astype(BF16), wpb_ref[...], preferred_element_type=F32)
    ga = jnp.concatenate([ga0_ref[0], ga1_ref[0]], axis=-1).astype(F32)
    gb = jnp.concatenate([gb0_ref[0], gb1_ref[0]], axis=-1).astype(F32)
    merged = a / (1.0 + jnp.exp(-ga)) + b / (1.0 + jnp.exp(-gb))
    mix = jnp.dot(merged.astype(BF16), wout_ref[...], preferred_element_type=F32)
    x1 = x_ref[0] + mod_ref[0, 2:3, :] * mix
    x1_ref[0] = x1
    h2_ref[0] = _modulated_norm(x1, gffn_ref[...], mod_ref[0, 4:5, :], mod_ref[0, 3:4, :]).astype(BF16)


def merge_proj(x, mod, p_nat, oa, outs, lses, wpa, wpb, wout, g_ffn, tm=512):
    B, S, _ = x.shape
    tm = min(tm, S)
    expand = np.zeros((LANES, DIL_COLS), np.float32)
    for h in range(DIL_HEADS):
        expand[h, h * DIL_HEAD_DIM:(h + 1) * DIL_HEAD_DIM] = 1.0
    expand = jnp.asarray(expand, BF16)

    def tok(w):
        return pl.BlockSpec((1, tm, w), lambda b, i: (b, i, 0))

    def res(w, r):
        return pl.BlockSpec((1, r, tm // r, w), lambda b, i: (b, 0, i, 0))

    def pcol(off):
        return pl.BlockSpec((1, tm, DIL_COLS), lambda b, i: (b, i, off // DIL_COLS))

    def whole(a):
        return pl.BlockSpec(a.shape, lambda b, i: (0,) * a.ndim)

    g = g_ffn.reshape(1, -1)
    dil = [d for _, d in DIL_GROUPS]
    return pl.pallas_call(
        _merge_kernel,
        grid=(B, S // tm),
        in_specs=[tok(D_MODEL), pl.BlockSpec((1, 6, D_MODEL), lambda b, i: (b, 0, 0)), tok(DA_V),
                  res(DIL_COLS, dil[0]), res(DIL_COLS, dil[1]), res(DIL_COLS, dil[2]),
                  res(LANES, dil[0]), res(LANES, dil[1]), res(LANES, dil[2]),
                  pcol(NAT_GA), pcol(NAT_GA + DIL_COLS), pcol(NAT_GB), pcol(NAT_GB + DIL_COLS),
                  whole(expand), whole(wpa), whole(wpb), whole(wout), whole(g)],
        out_specs=[tok(D_MODEL), tok(D_MODEL)],
        out_shape=[jax.ShapeDtypeStruct((B, S, D_MODEL), F32), jax.ShapeDtypeStruct((B, S, D_MODEL), BF16)],
        scratch_shapes=[pltpu.VMEM((DIL_COLS // LANES, tm, LANES), F32),
                        pltpu.VMEM((DIL_COLS // LANES, tm, LANES), F32),
                        pltpu.VMEM((1, tm, LANES), F32), pltpu.VMEM((1, tm, LANES), F32)],
        compiler_params=_cparams(("arbitrary", "arbitrary")),
        name="merge_proj",
    )(x, mod, oa, outs[0], outs[1], outs[2], lses[0], lses[1], lses[2], p_nat, p_nat, p_nat, p_nat,
      expand, wpa, wpb, wout, g)


def _top_values(s, count):
    vals = []
    for _ in range(count):
        m = jnp.max(s, axis=0, keepdims=True)
        vals.append(m)
        s = jnp.where(s >= m, -jnp.inf, s)
    return vals


def _top_values_ranked(s, count):
    vals = []
    rank = jnp.full(s.shape, float(count), F32)
    for i in range(count):
        m = jnp.max(s, axis=0, keepdims=True)
        vals.append(m)
        hit = s >= m
        rank = jnp.where(hit, float(i), rank)
        s = jnp.where(hit, -jnp.inf, s)
    return vals, rank


def _paired_bf16(x):
    bits = pltpu.bitcast(x.astype(BF16).astype(F32), jnp.uint32)
    return bits | (bits >> 16)


def _route_kernel(h_ref, wq_ref, k1_ref, k2_ref, r2_ref, e2_ref, cnt_ref, e1_ref):
    qp = jnp.dot(h_ref[...], wq_ref[...], preferred_element_type=F32)
    tm = qp.shape[0]
    row = lax.broadcasted_iota(jnp.int32, (PEER_PAIR_ROWS, tm), 0)
    for h in range(PEER_HEADS):
        q1 = qp[:, h * PEER_QUERY_DIM:h * PEER_QUERY_DIM + PEER_HALF].astype(BF16)
        q2 = qp[:, h * PEER_QUERY_DIM + PEER_HALF:(h + 1) * PEER_QUERY_DIM].astype(BF16)
        s1 = _dot_nt(k1_ref[h], q1)
        s2 = _dot_nt(k2_ref[h], q2)
        v1 = _top_values(s1, PEER_TOPK)
        v2, rank2 = _top_values_ranked(s2, PEER_TOPK)
        cand = jnp.full((PEER_PAIR_ROWS, tm), -jnp.inf, F32)
        for r, (i, j) in enumerate(PEER_PAIRS):
            cand = jnp.where(row == r, v1[i] + v2[j], cand)
        tau = _top_values(cand, PEER_TOPK)[-1]
        top = v1[0] + v2[0]
        z = jnp.sum(jnp.where(cand >= tau, jnp.exp(cand - top), 0.0), axis=0, keepdims=True)
        cnt = jnp.zeros(s1.shape, F32)
        for j in range(PEER_TOPK):
            cnt = cnt + jnp.where(s1 + v2[j] >= tau, 1.0, 0.0)
        r2_ref[h] = rank2.astype(BF16)
        e2_ref[h] = (jnp.exp(s2 - v2[0]) / z).astype(BF16)
        cnt_ref[h] = _paired_bf16(cnt)
        e1_ref[h] = _paired_bf16(0.5 * jnp.exp(s1 - v1[0]))


def peer_route(h2, wq, k1, k2, tm=256):
    T = h2.shape[0]
    tm = min(tm, T)
    half = jax.ShapeDtypeStruct((PEER_HEADS, N_KEYS, T), BF16)
    word = jax.ShapeDtypeStruct((PEER_HEADS, N_KEYS, T), jnp.uint32)
    bspec = pl.BlockSpec((PEER_HEADS, N_KEYS, tm), lambda i: (0, 0, i))
    kspec = pl.BlockSpec((PEER_HEADS, N_KEYS, PEER_HALF), lambda i: (0, 0, 0))
    return pl.pallas_call(
        _route_kernel,
        grid=(T // tm,),
        in_specs=[pl.BlockSpec((tm, D_MODEL), lambda i: (i, 0)),
                  pl.BlockSpec(wq.shape, lambda i: (0, 0)), kspec, kspec],
        out_specs=[bspec, bspec, bspec, bspec],
        out_shape=[half, half, word, word],
        compiler_params=_cparams(("arbitrary",)),
        name="peer_route",
    )(h2, wq, k1, k2)


PEER_TE = 2048
PEER_SLAB_GROUP = 8
PACK = 16


def _dense_kernel(h_ref, u_ref, vt_ref, r2_ref, e2_ref, cnt_ref, e1_ref, x1_ref, mod_ref, gfin_ref,
                  y_ref, acc_ref, a_ref):
    j = pl.program_id(1)
    tm = h_ref.shape[0]

    @pl.when(j == 0)
    def _():
        acc_ref[...] = jnp.zeros_like(acc_ref)

    def row_bf16(ref, h, a):
        words = jnp.broadcast_to(ref[h, pl.ds(a, 1), :], (8, tm))
        return pltpu.bitcast(words, BF16)

    zero = jnp.zeros((PACK, tm), BF16)
    group_rows = PEER_SLAB_GROUP * N_KEYS
    n_groups = PEER_TE // group_rows
    zts = [_dot_nt(u_ref[g * group_rows:(g + 1) * group_rows, :], h_ref[...]) for g in range(n_groups)]
    for sg in range(n_groups):
        experts = slice(sg * group_rows, (sg + 1) * group_rows)
        zt = zts[sg]
        first = [j * (PEER_TE // N_KEYS) + sg * PEER_SLAB_GROUP + al for al in range(PEER_SLAB_GROUP)]
        cnt_rows = [[row_bf16(cnt_ref, h, a) for a in first] for h in range(PEER_HEADS)]
        e1_rows = [[row_bf16(e1_ref, h, a) for a in first] for h in range(PEER_HEADS)]
        for c in range(N_KEYS // PACK):
            keys = slice(c * PACK, (c + 1) * PACK)
            w = [None] * PEER_SLAB_GROUP
            for h in range(PEER_HEADS):
                r2 = r2_ref[h, keys, :]
                e2 = e2_ref[h, keys, :]
                for al in range(PEER_SLAB_GROUP):
                    term = jnp.where(r2 < cnt_rows[h][al], e2, zero) * e1_rows[h][al]
                    w[al] = term if w[al] is None else w[al] + term
            for al in range(PEER_SLAB_GROUP):
                base = al * N_KEYS + c * PACK
                z = zt[base:base + PACK, :]
                a_ref[sg * group_rows + base:sg * group_rows + base + PACK, :] = (
                    (z * (1.0 + lax.erf(z * (2.0 ** -0.5)))).astype(BF16) * w[al])
        acc_ref[...] += jnp.dot(vt_ref[:, experts], a_ref[experts, :], preferred_element_type=F32)

    @pl.when(j == pl.num_programs(1) - 1)
    def _():
        x2 = x1_ref[...] + mod_ref[0, 5:6, :] * acc_ref[...].T
        ms = jnp.mean(x2 * x2, axis=-1, keepdims=True)
        y_ref[...] = x2 * lax.rsqrt(ms + EPS) * gfin_ref[...]


def peer_dense(h2, u_bf16, vt_bf16, routes, x1, mod, g_final, S, tm=512):
    T = h2.shape[0]
    tm = min(tm, S)
    r2, e2, cnt, e1 = routes
    rspec = pl.BlockSpec((PEER_HEADS, N_KEYS, tm), lambda i, j: (0, 0, i))
    tiles_per_row = S // tm
    return pl.pallas_call(
        _dense_kernel,
        grid=(T // tm, N_EXPERTS // PEER_TE),
        in_specs=[pl.BlockSpec((tm, D_MODEL), lambda i, j: (i, 0)),
                  pl.BlockSpec((PEER_TE, D_MODEL), lambda i, j: (j, 0)),
                  pl.BlockSpec((D_MODEL, PEER_TE), lambda i, j: (0, j)),
                  rspec, rspec, rspec, rspec,
                  pl.BlockSpec((tm, D_MODEL), lambda i, j: (i, 0)),
                  pl.BlockSpec((1, 6, D_MODEL), lambda i, j: (i // tiles_per_row, 0, 0)),
                  pl.BlockSpec((1, D_MODEL), lambda i, j: (0, 0))],
        out_specs=pl.BlockSpec((tm, D_MODEL), lambda i, j: (i, 0)),
        out_shape=jax.ShapeDtypeStruct((T, D_MODEL), F32),
        scratch_shapes=[pltpu.VMEM((D_MODEL, tm), F32), pltpu.VMEM((PEER_TE, tm), BF16)],
        compiler_params=_cparams(("arbitrary", "arbitrary")),
        name="peer_dense",
    )(h2, u_bf16, vt_bf16, r2, e2, cnt, e1, x1, mod, g_final.reshape(1, -1))


def _prepare_weights(w_in, w_proj_a, w_proj_b, w_out, rel_bias, w_query, sub_keys_1, sub_keys_2,
                     expert_u, expert_v):
    tabs = [rel_bias[:, :DA_HEADS].reshape(-1)]
    for gi in range(len(DIL_GROUPS)):
        lo = DA_HEADS + gi * DIL_HEADS
        tabs.append(rel_bias[:, lo:lo + DIL_HEADS].reshape(-1))
    w = w_in[0].astype(BF16)

    def group_cols(gi):
        return [w[:, off + gi * DIL_COLS:off + (gi + 1) * DIL_COLS] for off in (OFF_QB, OFF_KB, OFF_VB)]

    w_nat = jnp.concatenate([w[:, :OFF_QB]] + group_cols(0) + [w[:, OFF_GA:]], axis=1)
    w_groups = [jnp.concatenate(group_cols(gi), axis=1) for gi in range(1, len(DIL_GROUPS))]
    return dict(
        w_nat=w_nat, w_groups=w_groups, wpa=w_proj_a[0].astype(BF16), wpb=w_proj_b[0].astype(BF16),
        wout=w_out[0].astype(BF16), tabs=tabs, wq=w_query[0].astype(BF16),
        k1=sub_keys_1[0].astype(BF16), k2=sub_keys_2[0].astype(BF16),
        u=expert_u[0].astype(BF16), vt=expert_v[0].astype(BF16).T)


def _encoder(x, c, w_ada, b_ada, g_norm_mix, g_norm_ffn, lq1, lk1, lq2, lk2, g_subln, g_final, wts):
    B, S, _ = x.shape
    mod = ada_mod(c, w_ada[0], b_ada[0]).reshape(B, 6, D_MODEL)
    p_nat = in_proj(x, mod, g_norm_mix[0], wts["w_nat"], tn=NAT_COLS // 4)
    oa = diff_attn(p_nat, wts["tabs"][0], lq1[0], lk1[0], lq2[0], lk2[0], g_subln[0])
    outs, lses = [], []
    for gi, (_, dilation) in enumerate(DIL_GROUPS):
        if gi == 0:
            pg, qkv_col = p_nat.reshape(B, 1, S, NAT_COLS), NAT_Q0
        else:
            pg, qkv_col = in_proj(x, mod, g_norm_mix[0], wts["w_groups"][gi - 1], r=dilation), 0
        o_g, lse_g = dilated_group(pg, wts["tabs"][1 + gi], qkv_col, dilation)
        outs.append(o_g)
        lses.append(lse_g)
    x1, h2 = merge_proj(x, mod, p_nat, oa, outs, lses, wts["wpa"], wts["wpb"], wts["wout"], g_norm_ffn[0])
    h2 = h2.reshape(B * S, D_MODEL)
    routes = peer_route(h2, wts["wq"], wts["k1"], wts["k2"])
    y = peer_dense(h2, wts["u"], wts["vt"], routes, x1.reshape(B * S, D_MODEL), mod, g_final, S)
    return y.reshape(B, S, D_MODEL)


def kernel(x_prompt, x_sample, c_prompt, c_sample, w_ada, b_ada, g_norm_mix, g_norm_ffn, w_in, w_proj_a, w_proj_b, w_out, lambda_q1, lambda_k1, lambda_q2, lambda_k2, g_subln, rel_bias, w_query, sub_keys_1, sub_keys_2, expert_u, expert_v, g_final):
    wts = _prepare_weights(w_in, w_proj_a, w_proj_b, w_out, rel_bias, w_query, sub_keys_1, sub_keys_2,
                           expert_u, expert_v)
    args = (w_ada, b_ada, g_norm_mix, g_norm_ffn, lambda_q1, lambda_k1, lambda_q2, lambda_k2, g_subln, g_final, wts)
    return _encoder(x_prompt, c_prompt, *args), _encoder(x_sample, c_sample, *args)
```

```python
import functools
import math

import jax
import jax.numpy as jnp
import numpy as np
from jax import lax
from jax.experimental import pallas as pl
from jax.experimental.pallas import tpu as pltpu

F32 = jnp.float32
BF16 = jnp.bfloat16

D_MODEL = 1024
DA_HEADS = 8
DA_HEAD_DIM = 64
DA_QK = DA_HEADS * 2 * DA_HEAD_DIM
DA_V = DA_HEADS * 2 * DA_HEAD_DIM
DIL_GROUPS = ((128, 1), (512, 4), (2048, 16))
DIL_HEADS = 8
DIL_HEAD_DIM = 64
DIL_HALF_SPAN = 64
DIL_COLS = DIL_HEADS * DIL_HEAD_DIM
DIL_QKV = len(DIL_GROUPS) * DIL_COLS
IN_COLS = 2 * DA_QK + DA_V + 3 * DIL_QKV + 2 * D_MODEL
NUM_BUCKETS = 32
N_REL_HEADS = DA_HEADS + len(DIL_GROUPS) * DIL_HEADS
PEER_HEADS = 8
N_KEYS = 128
N_EXPERTS = N_KEYS * N_KEYS
PEER_QUERY_DIM = 256
PEER_HALF = PEER_QUERY_DIM // 2
PEER_TOPK = 16
EPS = 1e-6
NEG_BIG = -1e30
LAM_INIT = 0.8 - 0.6 * math.exp(-0.3 * 0)

LANES = 128
VMEM_LIMIT = 56 * 1024 * 1024

OFF_QA, OFF_KA, OFF_VA = 0, DA_QK, 2 * DA_QK
OFF_QB = 2 * DA_QK + DA_V
OFF_KB = OFF_QB + DIL_QKV
OFF_VB = OFF_KB + DIL_QKV
OFF_GA = OFF_VB + DIL_QKV
OFF_GB = OFF_GA + D_MODEL
NAT_Q0 = OFF_QB
NAT_GA = NAT_Q0 + 3 * DIL_COLS
NAT_GB = NAT_GA + D_MODEL
NAT_COLS = NAT_GB + D_MODEL

BUCKET_THRESHOLDS = (1, 2, 3, 4, 5, 6, 7, 8, 15, 27, 50, 91, 166, 305, 559)
BUCKET_CONST_FROM = BUCKET_THRESHOLDS[-1]


def _cparams(sem):
    return pltpu.CompilerParams(dimension_semantics=sem, vmem_limit_bytes=VMEM_LIMIT)


def _dot_nt(a, b):
    return lax.dot_general(a, b, (((1,), (1,)), ((), ())), preferred_element_type=F32)


def _bias_of_rel(rel, tab_ref, head, n_heads):
    n = jnp.abs(rel)
    neg = jnp.full(rel.shape, tab_ref[0 * n_heads + head], F32)
    pos = jnp.full(rel.shape, tab_ref[16 * n_heads + head], F32)
    for k, thr in enumerate(BUCKET_THRESHOLDS, start=1):
        ge = n >= thr
        neg = jnp.where(ge, tab_ref[k * n_heads + head], neg)
        pos = jnp.where(ge, tab_ref[(16 + k) * n_heads + head], pos)
    return jnp.where(rel > 0, pos, neg)


def _ada_kernel(c_ref, w_ref, b_ref, o_ref):
    c = c_ref[...]
    sc = c / (1.0 + jnp.exp(-c))
    o_ref[...] = jnp.dot(sc, w_ref[...], preferred_element_type=F32,
                         precision=lax.Precision.HIGHEST) + b_ref[...]


def ada_mod(c, w_ada, b_ada):
    B = c.shape[0]
    n = w_ada.shape[1] // D_MODEL
    return pl.pallas_call(
        _ada_kernel,
        grid=(n,),
        in_specs=[pl.BlockSpec((B, D_MODEL), lambda j: (0, 0)),
                  pl.BlockSpec((D_MODEL, D_MODEL), lambda j: (0, j)),
                  pl.BlockSpec((1, D_MODEL), lambda j: (0, j))],
        out_specs=pl.BlockSpec((B, D_MODEL), lambda j: (0, j)),
        out_shape=jax.ShapeDtypeStruct((B, n * D_MODEL), F32),
        compiler_params=_cparams(("arbitrary",)),
        name="ada_mod",
    )(c, w_ada, b_ada.reshape(1, -1))


def _modulated_norm(x, g, scale, shift):
    ms = jnp.mean(x * x, axis=-1, keepdims=True)
    return (x * lax.rsqrt(ms + EPS) * g) * (1.0 + scale) + shift


def _in_proj_kernel(x_ref, mod_ref, g_ref, w_ref, o_ref, h_ref, *hs_ref, r):
    tm = h_ref.shape[0]
    n = tm // r

    @pl.when(pl.program_id(2) == 0)
    def _():
        h = _modulated_norm(x_ref[0], g_ref[...], mod_ref[0, 1:2, :], mod_ref[0, 0:1, :])
        if r == 1:
            h_ref[...] = h.astype(BF16)
        else:
            for c in range(D_MODEL // LANES):
                hs_ref[0][c] = h[:, c * LANES:(c + 1) * LANES]
            for c in range(D_MODEL // LANES):
                h_ref[:, c * LANES:(c + 1) * LANES] = jnp.concatenate(
                    [hs_ref[0][c, pl.ds(rho, n, stride=r), :] for rho in range(r)], axis=0).astype(BF16)

    res = jnp.dot(h_ref[...], w_ref[...], preferred_element_type=F32).astype(BF16)
    if r == 1:
        o_ref[0] = res
    else:
        for rho in range(r):
            o_ref[0, rho] = res[rho * n:(rho + 1) * n, :]


def in_proj(x, mod, g_norm, w_bf16, r=1, tm=1024, tn=None):
    B, S, _ = x.shape
    N = w_bf16.shape[1]
    tn = N if tn is None else tn
    tm = min(tm, S)
    if r == 1:
        out_spec = pl.BlockSpec((1, tm, tn), lambda b, i, j: (b, i, j))
        out_shape = jax.ShapeDtypeStruct((B, S, N), BF16)
        scratch = [pltpu.VMEM((tm, D_MODEL), BF16)]
    else:
        out_spec = pl.BlockSpec((1, r, tm // r, tn), lambda b, i, j: (b, 0, i, j))
        out_shape = jax.ShapeDtypeStruct((B, r, S // r, N), BF16)
        scratch = [pltpu.VMEM((tm, D_MODEL), BF16), pltpu.VMEM((D_MODEL // LANES, tm, LANES), F32)]
    return pl.pallas_call(
        functools.partial(_in_proj_kernel, r=r),
        grid=(B, S // tm, N // tn),
        in_specs=[pl.BlockSpec((1, tm, D_MODEL), lambda b, i, j: (b, i, 0)),
                  pl.BlockSpec((1, 6, D_MODEL), lambda b, i, j: (b, 0, 0)),
                  pl.BlockSpec((1, D_MODEL), lambda b, i, j: (0, 0)),
                  pl.BlockSpec((D_MODEL, tn), lambda b, i, j: (0, j))],
        out_specs=out_spec,
        out_shape=out_shape,
        scratch_shapes=scratch,
        compiler_params=_cparams(("arbitrary", "arbitrary", "arbitrary")),
        name=f"in_proj_r{r}",
    )(x, mod, g_norm.reshape(1, -1), w_bf16)


DA_TQ = 512
DA_TK = 512
DA_D_LO = -(-(BUCKET_CONST_FROM - 1 + DA_TK) // DA_TQ)
DA_D_HI = -(-(BUCKET_CONST_FROM - 1 + DA_TQ) // DA_TQ)
DA_NBIAS = DA_D_LO + DA_D_HI + 1
DA_VALUE_PARTS = 2
DA_VROWS = 144
LOG2E = 1.4426950408889634


def _da_kernel(tab_ref, q_ref, k_ref, v_ref, lq1_ref, lk1_ref, lq2_ref, lk2_ref, gsub_ref, o_ref,
               bias_ref, vt_ref, s_ref, p_ref, *, n_chunks):
    TQ, TK = DA_TQ, DA_TK
    head = pl.program_id(0)
    qi = pl.program_id(2)

    @pl.when((pl.program_id(1) == 0) & (qi == 0))
    def _():
        row = lax.broadcasted_iota(jnp.int32, (TK, TQ), 0)
        col = lax.broadcasted_iota(jnp.int32, (TK, TQ), 1)
        for d in range(DA_NBIAS):
            bias_ref[d] = _bias_of_rel((d - DA_D_LO) * TQ + row - col, tab_ref, head, DA_HEADS) * LOG2E

    @pl.when(qi == 0)
    def _():
        extra = lax.broadcasted_iota(jnp.int32, (DA_VROWS - LANES, TK), 0)
        ones_row = jnp.where(extra == 0, 1.0, 0.0).astype(BF16)
        for c in range(n_chunks):
            vt_ref[0:LANES, c * TK:(c + 1) * TK] = v_ref[0, c * TK:(c + 1) * TK, :].astype(F32).T.astype(BF16)
            vt_ref[LANES:DA_VROWS, c * TK:(c + 1) * TK] = ones_row

    q = q_ref[0].astype(F32) * (DA_HEAD_DIM ** -0.5 * LOG2E)
    lane = lax.broadcasted_iota(jnp.int32, q.shape, 1)
    q1 = jnp.where(lane < DA_HEAD_DIM, q, 0.0).astype(BF16)
    q2 = jnp.where(lane >= DA_HEAD_DIM, q, 0.0).astype(BF16)

    def score_chunk(kc, carry):
        m1, m2 = carry
        k = k_ref[0, pl.ds(pl.multiple_of(kc * TK, TK), TK), :]
        d = kc * (TK // TQ) - qi
        bias = bias_ref[jnp.clip(d, -DA_D_LO, DA_D_HI) + DA_D_LO]
        s1 = _dot_nt(k, q1) + bias
        s2 = _dot_nt(k, q2) + bias
        s_ref[0, kc] = s1
        s_ref[1, kc] = s2
        return (jnp.maximum(m1, jnp.max(s1, axis=0, keepdims=True)),
                jnp.maximum(m2, jnp.max(s2, axis=0, keepdims=True)))

    neg = jnp.full((1, TQ), -jnp.inf, F32)
    m1, m2 = lax.fori_loop(0, n_chunks, score_chunk, (neg, neg), unroll=4)

    per_part = n_chunks // DA_VALUE_PARTS
    a1 = a2 = None
    for part in range(DA_VALUE_PARTS):
        for c in range(part * per_part, (part + 1) * per_part):
            p_ref[0, c * TK:(c + 1) * TK, :] = jnp.exp2((s_ref[0, c] - m1).astype(BF16))
            p_ref[1, c * TK:(c + 1) * TK, :] = jnp.exp2((s_ref[1, c] - m2).astype(BF16))
        keys = slice(part * per_part * TK, (part + 1) * per_part * TK)
        d1 = jnp.dot(vt_ref[:, keys], p_ref[0, keys, :], preferred_element_type=F32)
        d2 = jnp.dot(vt_ref[:, keys], p_ref[1, keys, :], preferred_element_type=F32)
        a1 = d1 if a1 is None else a1 + d1
        a2 = d2 if a2 is None else a2 + d2

    lam = (jnp.exp(jnp.sum(lq1_ref[...] * lk1_ref[...], axis=-1, keepdims=True))
           - jnp.exp(jnp.sum(lq2_ref[...] * lk2_ref[...], axis=-1, keepdims=True)) + LAM_INIT)
    ot = a1[0:LANES] / a1[LANES:LANES + 1] - lam * (a2[0:LANES] / a2[LANES:LANES + 1])
    ms = jnp.mean(ot * ot, axis=0, keepdims=True)
    y = ot * lax.rsqrt(ms + EPS) * gsub_ref[...] * (1.0 - LAM_INIT)
    o_ref[0] = y.T.astype(BF16)


def diff_attn(p, tab, lq1, lk1, lq2, lk2, g_subln):
    B, S, _ = p.shape
    TQ, TK = DA_TQ, DA_TK
    n_chunks = S // TK
    small = pl.BlockSpec((1, DA_HEAD_DIM), lambda h, b, i: (0, 0))
    return pl.pallas_call(
        functools.partial(_da_kernel, n_chunks=n_chunks),
        grid=(DA_HEADS, B, S // TQ),
        in_specs=[pl.BlockSpec(memory_space=pltpu.SMEM),
                  pl.BlockSpec((1, TQ, LANES), lambda h, b, i: (b, i, OFF_QA // LANES + h)),
                  pl.BlockSpec((1, S, LANES), lambda h, b, i: (b, 0, OFF_KA // LANES + h)),
                  pl.BlockSpec((1, S, LANES), lambda h, b, i: (b, 0, OFF_VA // LANES + h)),
                  small, small, small, small,
                  pl.BlockSpec((2 * DA_HEAD_DIM, 1), lambda h, b, i: (0, 0))],
        out_specs=pl.BlockSpec((1, TQ, LANES), lambda h, b, i: (b, i, h)),
        out_shape=jax.ShapeDtypeStruct((B, S, DA_V), BF16),
        scratch_shapes=[pltpu.VMEM((DA_NBIAS, TK, TQ), F32),
                        pltpu.VMEM((DA_VROWS, S), BF16),
                        pltpu.VMEM((2, n_chunks, TK, TQ), F32),
                        pltpu.VMEM((2, S, TQ), BF16)],
        compiler_params=_cparams(("arbitrary", "arbitrary", "arbitrary")),
        name="diff_attn",
    )(tab, p, p, p, lq1.reshape(1, -1), lk1.reshape(1, -1), lq2.reshape(1, -1), lk2.reshape(1, -1),
      g_subln.reshape(-1, 1))


DIL_TQ = 128
DIL_PAD = DIL_HALF_SPAN
DIL_WIN = DIL_TQ + 2 * DIL_PAD


def _dil_kernel(tab_ref, q_ref, k_ref, v_ref, o_ref, lse_ref, kpad_ref, vpad_ref, mask_ref, *, L, LQ, dilation):
    TQ, W, hs = DIL_TQ, DIL_WIN, DIL_HALF_SPAN
    n_tiles = L // TQ
    qblock = pl.program_id(2)

    @pl.when((pl.program_id(0) == 0) & (pl.program_id(1) == 0) & (qblock == 0))
    def _():
        row = lax.broadcasted_iota(jnp.int32, (TQ, W), 0)
        col = lax.broadcasted_iota(jnp.int32, (TQ, W), 1)
        rel = col - DIL_PAD - row
        band = jnp.abs(rel) <= hs
        for variant, qs in enumerate((0, TQ, L - TQ)):
            kpos = qs - DIL_PAD + col
            ok = band & (kpos >= 0) & (kpos < L)
            for h in range(DIL_HEADS):
                mask_ref[variant, h] = jnp.where(ok, _bias_of_rel(dilation * rel, tab_ref, h, DIL_HEADS), NEG_BIG)

    @pl.when(qblock == 0)
    def _():
        pad = jnp.zeros((DIL_PAD, DIL_COLS), BF16)
        for ref, src in ((kpad_ref, k_ref), (vpad_ref, v_ref)):
            ref[0:DIL_PAD, :] = pad
            ref[DIL_PAD + L:2 * DIL_PAD + L, :] = pad
            ref[DIL_PAD:DIL_PAD + L, :] = src[0]

    lane = lax.broadcasted_iota(jnp.int32, (TQ, LANES), 1)
    low = lane < DIL_HEAD_DIM

    def tile(t, _):
        qs = pl.multiple_of(t * TQ, TQ)
        tg = qblock * (LQ // TQ) + t
        ks = pl.multiple_of(tg * TQ, TQ)
        variant = jnp.where(tg == 0, 0, jnp.where(tg == n_tiles - 1, 2, 1))
        lse_tile = jnp.zeros((TQ, LANES), F32)
        scores = []
        for h in range(DIL_HEADS):
            cols = slice(h // 2 * LANES, (h // 2 + 1) * LANES)
            q = q_ref[0, pl.ds(qs, TQ), cols].astype(F32) * (DIL_HEAD_DIM ** -0.5)
            qm = jnp.where(low if h % 2 == 0 else ~low, q, 0.0).astype(BF16)
            scores.append(_dot_nt(qm, kpad_ref[pl.ds(ks, W), cols]) + mask_ref[variant, h])
        probs, dens = [], []
        for h in range(DIL_HEADS):
            m = jnp.max(scores[h], axis=-1, keepdims=True)
            pr = jnp.exp(scores[h] - m)
            den = jnp.sum(pr, axis=-1, keepdims=True)
            probs.append(pr.astype(BF16))
            dens.append(den)
            lse_tile = jnp.where(lane == h, m + jnp.log(den), lse_tile)
        for hp in range(DIL_HEADS // 2):
            cols = slice(hp * LANES, (hp + 1) * LANES)
            vw = vpad_ref[pl.ds(ks, W), cols]
            outs = [jnp.dot(probs[2 * hp + e], vw, preferred_element_type=F32) / dens[2 * hp + e] for e in range(2)]
            o_ref[0, pl.ds(qs, TQ), cols] = jnp.where(low, outs[0], outs[1])
        lse_ref[0, pl.ds(qs, TQ), :] = lse_tile
        return 0

    lax.fori_loop(0, LQ // TQ, tile, 0)


def dilated_group(pg, tab, qkv_col, dilation, lq=1024):
    B, r, L, _ = pg.shape
    assert r == dilation and L >= 2 * DIL_TQ and L % DIL_TQ == 0
    LQ = min(lq, L)

    def col(which, rows, tiled):
        cb = qkv_col // DIL_COLS + which
        return pl.BlockSpec((1, None, rows, DIL_COLS), lambda b, rho, t: (b, rho, t if tiled else 0, cb))

    return pl.pallas_call(
        functools.partial(_dil_kernel, L=L, LQ=LQ, dilation=r),
        grid=(B, r, L // LQ),
        in_specs=[pl.BlockSpec(memory_space=pltpu.SMEM),
                  col(0, LQ, True), col(1, L, False), col(2, L, False)],
        out_specs=[pl.BlockSpec((1, None, LQ, DIL_COLS), lambda b, rho, t: (b, rho, t, 0)),
                   pl.BlockSpec((1, None, LQ, LANES), lambda b, rho, t: (b, rho, t, 0))],
        out_shape=[jax.ShapeDtypeStruct((B, r, L, DIL_COLS), F32),
                   jax.ShapeDtypeStruct((B, r, L, LANES), F32)],
        scratch_shapes=[pltpu.VMEM((L + 2 * DIL_PAD, DIL_COLS), BF16),
                        pltpu.VMEM((L + 2 * DIL_PAD, DIL_COLS), BF16),
                        pltpu.VMEM((3, DIL_HEADS, DIL_TQ, DIL_WIN), F32)],
        compiler_params=_cparams(("arbitrary", "arbitrary", "arbitrary")),
        name=f"dilated_r{r}",
    )(tab, pg, pg, pg)


def _split_dot(w, e_ref):
    hi = w.astype(BF16)
    lo = (w - hi.astype(F32)).astype(BF16)
    e = e_ref[...]
    return jnp.dot(hi, e, preferred_element_type=F32) + jnp.dot(lo, e, preferred_element_type=F32)


def _token_order(src_ref, tmp_ref, r):
    if r == 1:
        return src_ref[0, 0]
    n, width = src_ref.shape[2], src_ref.shape[3]
    for c in range(width // LANES):
        for rho in range(r):
            tmp_ref[c, pl.ds(rho, n, stride=r), :] = src_ref[0, rho, :, c * LANES:(c + 1) * LANES]
    return jnp.concatenate([tmp_ref[c] for c in range(width // LANES)], axis=1)


def _merge_kernel(x_ref, mod_ref, oa_ref, o0_ref, o1_ref, o2_ref, l0_ref, l1_ref, l2_ref,
                  ga0_ref, ga1_ref, gb0_ref, gb1_ref, e_ref, wpa_ref, wpb_ref, wout_ref, gffn_ref,
                  x1_ref, h2_ref, to1_ref, to2_ref, tl1_ref, tl2_ref):
    dil = [d for _, d in DIL_GROUPS]
    a = jnp.dot(oa_ref[0], wpa_ref[...], preferred_element_type=F32)
    l0 = _token_order(l0_ref, None, dil[0])
    l1 = _token_order(l1_ref, tl1_ref, dil[1])
    l2 = _token_order(l2_ref, tl2_ref, dil[2])
    mx = jnp.maximum(jnp.maximum(l0, l1), l2)
    e0, e1, e2 = jnp.exp(l0 - mx), jnp.exp(l1 - mx), jnp.exp(l2 - mx)
    den = e0 + e1 + e2
    ob = (_token_order(o0_ref, None, dil[0]) * _split_dot(e0 / den, e_ref)
          + _token_order(o1_ref, to1_ref, dil[1]) * _split_dot(e1 / den, e_ref)
          + _token_order(o2_ref, to2_ref, dil[2]) * _split_dot(e2 / den, e_ref))
    b = jnp.dot(ob.astype(BF16), wpb_ref[...], preferred_element_type=F32)
    ga = jnp.concatenate([ga0_ref[0], ga1_ref[0]], axis=-1).astype(F32)
    gb = jnp.concatenate([gb0_ref[0], gb1_ref[0]], axis=-1).astype(F32)
    merged = a / (1.0 + jnp.exp(-ga)) + b / (1.0 + jnp.exp(-gb))
    mix = jnp.dot(merged.astype(BF16), wout_ref[...], preferred_element_type=F32)
    x1 = x_ref[0] + mod_ref[0, 2:3, :] * mix
    x1_ref[0] = x1
    h2_ref[0] = _modulated_norm(x1, gffn_ref[...], mod_ref[0, 4:5, :], mod_ref[0, 3:4, :]).astype(BF16)


def merge_proj(x, mod, p_nat, oa, outs, lses, wpa, wpb, wout, g_ffn, tm=512):
    B, S, _ = x.shape
    tm = min(tm, S)
    expand = np.zeros((LANES, DIL_COLS), np.float32)
    for h in range(DIL_HEADS):
        expand[h, h * DIL_HEAD_DIM:(h + 1) * DIL_HEAD_DIM] = 1.0
    expand = jnp.asarray(expand, BF16)

    def tok(w):
        return pl.BlockSpec((1, tm, w), lambda b, i: (b, i, 0))

    def res(w, r):
        return pl.BlockSpec((1, r, tm // r, w), lambda b, i: (b, 0, i, 0))

    def pcol(off):
        return pl.BlockSpec((1, tm, DIL_COLS), lambda b, i: (b, i, off // DIL_COLS))

    def whole(a):
        return pl.BlockSpec(a.shape, lambda b, i: (0,) * a.ndim)

    g = g_ffn.reshape(1, -1)
    dil = [d for _, d in DIL_GROUPS]
    return pl.pallas_call(
        _merge_kernel,
        grid=(B, S // tm),
        in_specs=[tok(D_MODEL), pl.BlockSpec((1, 6, D_MODEL), lambda b, i: (b, 0, 0)), tok(DA_V),
                  res(DIL_COLS, dil[0]), res(DIL_COLS, dil[1]), res(DIL_COLS, dil[2]),
                  res(LANES, dil[0]), res(LANES, dil[1]), res(LANES, dil[2]),
                  pcol(NAT_GA), pcol(NAT_GA + DIL_COLS), pcol(NAT_GB), pcol(NAT_GB + DIL_COLS),
                  whole(expand), whole(wpa), whole(wpb), whole(wout), whole(g)],
        out_specs=[tok(D_MODEL), tok(D_MODEL)],
        out_shape=[jax.ShapeDtypeStruct((B, S, D_MODEL), F32), jax.ShapeDtypeStruct((B, S, D_MODEL), BF16)],
        scratch_shapes=[pltpu.VMEM((DIL_COLS // LANES, tm, LANES), F32),
                        pltpu.VMEM((DIL_COLS // LANES, tm, LANES), F32),
                        pltpu.VMEM((1, tm, LANES), F32), pltpu.VMEM((1, tm, LANES), F32)],
        compiler_params=_cparams(("arbitrary", "arbitrary")),
        name="merge_proj",
    )(x, mod, oa, outs[0], outs[1], outs[2], lses[0], lses[1], lses[2], p_nat, p_nat, p_nat, p_nat,
      expand, wpa, wpb, wout, g)


def _top_values(s, count, ranks=False):
    row = lax.broadcasted_iota(jnp.int32, (count, s.shape[1]), 0)
    rows = []
    top = jnp.zeros((count, s.shape[1]), F32)
    rank = jnp.full(s.shape, float(count), F32)
    for i in range(count):
        m = jnp.max(s, axis=0, keepdims=True)
        rows.append(m)
        top = jnp.where(row == i, m, top)
        hit = s >= m
        if ranks:
            rank = jnp.where(hit, float(i), rank)
        s = jnp.where(hit, -jnp.inf, s)
    return (rows, top, rank) if ranks else (rows, top)


def _paired_bf16(x):
    bits = pltpu.bitcast(x.astype(BF16).astype(F32), jnp.uint32)
    return bits | (bits >> 16)


def _route_kernel(h_ref, wq_ref, k1_ref, k2_ref, r2_ref, e2_ref, cnt_ref, e1_ref):
    qp = jnp.dot(h_ref[...], wq_ref[...], preferred_element_type=F32)
    tm = qp.shape[0]
    K, half = PEER_TOPK, PEER_TOPK // 2
    first_row = lax.broadcasted_iota(jnp.int32, (K, tm), 0) == 0
    for h in range(PEER_HEADS):
        q1 = qp[:, h * PEER_QUERY_DIM:h * PEER_QUERY_DIM + PEER_HALF].astype(BF16)
        q2 = qp[:, h * PEER_QUERY_DIM + PEER_HALF:(h + 1) * PEER_QUERY_DIM].astype(BF16)
        s1 = _dot_nt(k1_ref[h], q1)
        s2 = _dot_nt(k2_ref[h], q2)
        r1, v1 = _top_values(s1, K)
        r2, v2, rank2 = _top_values(s2, K, ranks=True)
        pieces = [v1 + r2[0]] + [v1[0:half] + r2[j] for j in range(1, half)] + [v2[half:K] + r1[0]]
        cand = jnp.concatenate(pieces, axis=0)
        tau = _top_values(cand, K)[0][K - 1]
        top = r1[0] + r2[0]
        z = jnp.sum(jnp.where(cand >= tau, jnp.exp(cand - top), 0.0), axis=0, keepdims=True)
        reach = [jnp.where(p >= tau, 1.0, 0.0) for p in pieces]
        low = reach[1]
        for p in reach[2:half]:
            low = low + p
        tail = jnp.sum(reach[half], axis=0, keepdims=True)
        cnt_rank = reach[0] + jnp.concatenate([low, jnp.zeros_like(low)], axis=0) + jnp.where(first_row, tail, 0.0)
        cnt = jnp.zeros(s1.shape, F32)
        for i in range(K):
            cnt = jnp.where(s1 == r1[i], cnt_rank[i:i + 1], cnt)
        r2_ref[h] = rank2.astype(BF16)
        e2_ref[h] = (jnp.exp(s2 - r2[0]) / z).astype(BF16)
        cnt_ref[h] = _paired_bf16(cnt)
        e1_ref[h] = _paired_bf16(0.5 * jnp.exp(s1 - r1[0]))


def peer_route(h2, wq, k1, k2, tm=256):
    T = h2.shape[0]
    tm = min(tm, T)
    half = jax.ShapeDtypeStruct((PEER_HEADS, N_KEYS, T), BF16)
    word = jax.ShapeDtypeStruct((PEER_HEADS, N_KEYS, T), jnp.uint32)
    bspec = pl.BlockSpec((PEER_HEADS, N_KEYS, tm), lambda i: (0, 0, i))
    kspec = pl.BlockSpec((PEER_HEADS, N_KEYS, PEER_HALF), lambda i: (0, 0, 0))
    return pl.pallas_call(
        _route_kernel,
        grid=(T // tm,),
        in_specs=[pl.BlockSpec((tm, D_MODEL), lambda i: (i, 0)),
                  pl.BlockSpec(wq.shape, lambda i: (0, 0)), kspec, kspec],
        out_specs=[bspec, bspec, bspec, bspec],
        out_shape=[half, half, word, word],
        compiler_params=_cparams(("arbitrary",)),
        name="peer_route",
    )(h2, wq, k1, k2)


PEER_TE = 2048
PEER_SLAB_GROUP = 8
PACK = 16


def _dense_kernel(h_ref, u_ref, vt_ref, r2_ref, e2_ref, cnt_ref, e1_ref, x1_ref, mod_ref, gfin_ref,
                  y_ref, acc_ref, a_ref):
    j = pl.program_id(1)
    tm = h_ref.shape[0]

    @pl.when(j == 0)
    def _():
        acc_ref[...] = jnp.zeros_like(acc_ref)

    def rows_bf16(ref, h, first):
        block = ref[h, pl.ds(first, PEER_SLAB_GROUP), :]
        return [pltpu.bitcast(jnp.broadcast_to(block[al:al + 1, :], (8, tm)), BF16) for al in range(PEER_SLAB_GROUP)]

    zero = jnp.zeros((PACK, tm), BF16)
    group_rows = PEER_SLAB_GROUP * N_KEYS
    n_groups = PEER_TE // group_rows
    zts = [_dot_nt(u_ref[g * group_rows:(g + 1) * group_rows, :], h_ref[...]) for g in range(n_groups)]
    for sg in range(n_groups):
        experts = slice(sg * group_rows, (sg + 1) * group_rows)
        zt = zts[sg]
        first = pl.multiple_of(j * (PEER_TE // N_KEYS) + sg * PEER_SLAB_GROUP, PEER_SLAB_GROUP)
        cnt_rows = [rows_bf16(cnt_ref, h, first) for h in range(PEER_HEADS)]
        e1_rows = [rows_bf16(e1_ref, h, first) for h in range(PEER_HEADS)]
        for c in range(N_KEYS // PACK):
            keys = slice(c * PACK, (c + 1) * PACK)
            w = [None] * PEER_SLAB_GROUP
            for h in range(PEER_HEADS):
                r2 = r2_ref[h, keys, :]
                e2 = e2_ref[h, keys, :]
                for al in range(PEER_SLAB_GROUP):
                    term = jnp.where(r2 < cnt_rows[h][al], e2, zero) * e1_rows[h][al]
                    w[al] = term if w[al] is None else w[al] + term
            for al in range(PEER_SLAB_GROUP):
                base = al * N_KEYS + c * PACK
                z = zt[base:base + PACK, :]
                a_ref[sg * group_rows + base:sg * group_rows + base + PACK, :] = (
                    (z * (1.0 + lax.erf(z * (2.0 ** -0.5)))).astype(BF16) * w[al])
        acc_ref[...] += jnp.dot(vt_ref[:, experts], a_ref[experts, :], preferred_element_type=F32)

    @pl.when(j == pl.num_programs(1) - 1)
    def _():
        x2 = x1_ref[...] + mod_ref[0, 5:6, :] * acc_ref[...].T
        ms = jnp.mean(x2 * x2, axis=-1, keepdims=True)
        y_ref[...] = x2 * lax.rsqrt(ms + EPS) * gfin_ref[...]


def peer_dense(h2, u_bf16, vt_bf16, routes, x1, mod, g_final, S, tm=512):
    T = h2.shape[0]
    tm = min(tm, S)
    r2, e2, cnt, e1 = routes
    rspec = pl.BlockSpec((PEER_HEADS, N_KEYS, tm), lambda i, j: (0, 0, i))
    tiles_per_row = S // tm
    return pl.pallas_call(
        _dense_kernel,
        grid=(T // tm, N_EXPERTS // PEER_TE),
        in_specs=[pl.BlockSpec((tm, D_MODEL), lambda i, j: (i, 0)),
                  pl.BlockSpec((PEER_TE, D_MODEL), lambda i, j: (j, 0)),
                  pl.BlockSpec((D_MODEL, PEER_TE), lambda i, j: (0, j)),
                  rspec, rspec, rspec, rspec,
                  pl.BlockSpec((tm, D_MODEL), lambda i, j: (i, 0)),
                  pl.BlockSpec((1, 6, D_MODEL), lambda i, j: (i // tiles_per_row, 0, 0)),
                  pl.BlockSpec((1, D_MODEL), lambda i, j: (0, 0))],
        out_specs=pl.BlockSpec((tm, D_MODEL), lambda i, j: (i, 0)),
        out_shape=jax.ShapeDtypeStruct((T, D_MODEL), F32),
        scratch_shapes=[pltpu.VMEM((D_MODEL, tm), F32), pltpu.VMEM((PEER_TE, tm), BF16)],
        compiler_params=_cparams(("arbitrary", "arbitrary")),
        name="peer_dense",
    )(h2, u_bf16, vt_bf16, r2, e2, cnt, e1, x1, mod, g_final.reshape(1, -1))


def _prepare_weights(w_in, w_proj_a, w_proj_b, w_out, rel_bias, w_query, sub_keys_1, sub_keys_2,
                     expert_u, expert_v):
    tabs = [rel_bias[:, :DA_HEADS].reshape(-1)]
    for gi in range(len(DIL_GROUPS)):
        lo = DA_HEADS + gi * DIL_HEADS
        tabs.append(rel_bias[:, lo:lo + DIL_HEADS].reshape(-1))
    w = w_in[0].astype(BF16)

    def group_cols(gi):
        return [w[:, off + gi * DIL_COLS:off + (gi + 1) * DIL_COLS] for off in (OFF_QB, OFF_KB, OFF_VB)]

    w_nat = jnp.concatenate([w[:, :OFF_QB]] + group_cols(0) + [w[:, OFF_GA:]], axis=1)
    w_groups = [jnp.concatenate(group_cols(gi), axis=1) for gi in range(1, len(DIL_GROUPS))]
    return dict(
        w_nat=w_nat, w_groups=w_groups, wpa=w_proj_a[0].astype(BF16), wpb=w_proj_b[0].astype(BF16),
        wout=w_out[0].astype(BF16), tabs=tabs, wq=w_query[0].astype(BF16),
        k1=sub_keys_1[0].astype(BF16), k2=sub_keys_2[0].astype(BF16),
        u=expert_u[0].astype(BF16), vt=expert_v[0].astype(BF16).T)


def _encoder(x, c, w_ada, b_ada, g_norm_mix, g_norm_ffn, lq1, lk1, lq2, lk2, g_subln, g_final, wts):
    B, S, _ = x.shape
    mod = ada_mod(c, w_ada[0], b_ada[0]).reshape(B, 6, D_MODEL)
    p_nat = in_proj(x, mod, g_norm_mix[0], wts["w_nat"], tn=NAT_COLS // 4)
    oa = diff_attn(p_nat, wts["tabs"][0], lq1[0], lk1[0], lq2[0], lk2[0], g_subln[0])
    outs, lses = [], []
    for gi, (_, dilation) in enumerate(DIL_GROUPS):
        if gi == 0:
            pg, qkv_col = p_nat.reshape(B, 1, S, NAT_COLS), NAT_Q0
        else:
            pg, qkv_col = in_proj(x, mod, g_norm_mix[0], wts["w_groups"][gi - 1], r=dilation), 0
        o_g, lse_g = dilated_group(pg, wts["tabs"][1 + gi], qkv_col, dilation)
        outs.append(o_g)
        lses.append(lse_g)
    x1, h2 = merge_proj(x, mod, p_nat, oa, outs, lses, wts["wpa"], wts["wpb"], wts["wout"], g_norm_ffn[0])
    h2 = h2.reshape(B * S, D_MODEL)
    routes = peer_route(h2, wts["wq"], wts["k1"], wts["k2"])
    y = peer_dense(h2, wts["u"], wts["vt"], routes, x1.reshape(B * S, D_MODEL), mod, g_final, S)
    return y.reshape(B, S, D_MODEL)


def kernel(x_prompt, x_sample, c_prompt, c_sample, w_ada, b_ada, g_norm_mix, g_norm_ffn, w_in, w_proj_a, w_proj_b, w_out, lambda_q1, lambda_k1, lambda_q2, lambda_k2, g_subln, rel_bias, w_query, sub_keys_1, sub_keys_2, expert_u, expert_v, g_final):
    wts = _prepare_weights(w_in, w_proj_a, w_proj_b, w_out, rel_bias, w_query, sub_keys_1, sub_keys_2,
                           expert_u, expert_v)
    args = (w_ada, b_ada, g_norm_mix, g_norm_ffn, lambda_q1, lambda_k1, lambda_q2, lambda_k2, g_subln, g_final, wts)
    return _encoder(x_prompt, c_prompt, *args), _encoder(x_sample, c_sample, *args)
```

```python
import functools
import math

import jax
import jax.numpy as jnp
import numpy as np
from jax import lax
from jax.experimental import pallas as pl
from jax.experimental.pallas import tpu as pltpu

F32 = jnp.float32
BF16 = jnp.bfloat16

D_MODEL = 1024
DA_HEADS = 8
DA_HEAD_DIM = 64
DA_QK = DA_HEADS * 2 * DA_HEAD_DIM
DA_V = DA_HEADS * 2 * DA_HEAD_DIM
DIL_GROUPS = ((128, 1), (512, 4), (2048, 16))
DIL_HEADS = 8
DIL_HEAD_DIM = 64
DIL_HALF_SPAN = 64
DIL_COLS = DIL_HEADS * DIL_HEAD_DIM
DIL_QKV = len(DIL_GROUPS) * DIL_COLS
IN_COLS = 2 * DA_QK + DA_V + 3 * DIL_QKV + 2 * D_MODEL
NUM_BUCKETS = 32
N_REL_HEADS = DA_HEADS + len(DIL_GROUPS) * DIL_HEADS
PEER_HEADS = 8
N_KEYS = 128
N_EXPERTS = N_KEYS * N_KEYS
PEER_QUERY_DIM = 256
PEER_HALF = PEER_QUERY_DIM // 2
PEER_TOPK = 16
EPS = 1e-6
NEG_BIG = -1e30
LAM_INIT = 0.8 - 0.6 * math.exp(-0.3 * 0)

LANES = 128
VMEM_LIMIT = 56 * 1024 * 1024

OFF_QA, OFF_KA, OFF_VA = 0, DA_QK, 2 * DA_QK
OFF_QB = 2 * DA_QK + DA_V
OFF_KB = OFF_QB + DIL_QKV
OFF_VB = OFF_KB + DIL_QKV
OFF_GA = OFF_VB + DIL_QKV
OFF_GB = OFF_GA + D_MODEL
NAT_Q0 = OFF_QB
NAT_GA = NAT_Q0 + 3 * DIL_COLS
NAT_GB = NAT_GA + D_MODEL
NAT_COLS = NAT_GB + D_MODEL

BUCKET_THRESHOLDS = (1, 2, 3, 4, 5, 6, 7, 8, 15, 27, 50, 91, 166, 305, 559)
BUCKET_CONST_FROM = BUCKET_THRESHOLDS[-1]


def _cparams(sem):
    return pltpu.CompilerParams(dimension_semantics=sem, vmem_limit_bytes=VMEM_LIMIT)


def _dot_nt(a, b):
    return lax.dot_general(a, b, (((1,), (1,)), ((), ())), preferred_element_type=F32)


def _bias_of_rel(rel, tab_ref, head, n_heads):
    n = jnp.abs(rel)
    neg = jnp.full(rel.shape, tab_ref[0 * n_heads + head], F32)
    pos = jnp.full(rel.shape, tab_ref[16 * n_heads + head], F32)
    for k, thr in enumerate(BUCKET_THRESHOLDS, start=1):
        ge = n >= thr
        neg = jnp.where(ge, tab_ref[k * n_heads + head], neg)
        pos = jnp.where(ge, tab_ref[(16 + k) * n_heads + head], pos)
    return jnp.where(rel > 0, pos, neg)


def _ada_kernel(c_ref, w_ref, b_ref, o_ref):
    c = c_ref[...]
    sc = c / (1.0 + jnp.exp(-c))
    o_ref[...] = jnp.dot(sc, w_ref[...], preferred_element_type=F32,
                         precision=lax.Precision.HIGHEST) + b_ref[...]


def ada_mod(c, w_ada, b_ada):
    B = c.shape[0]
    n = w_ada.shape[1] // D_MODEL
    return pl.pallas_call(
        _ada_kernel,
        grid=(n,),
        in_specs=[pl.BlockSpec((B, D_MODEL), lambda j: (0, 0)),
                  pl.BlockSpec((D_MODEL, D_MODEL), lambda j: (0, j)),
                  pl.BlockSpec((1, D_MODEL), lambda j: (0, j))],
        out_specs=pl.BlockSpec((B, D_MODEL), lambda j: (0, j)),
        out_shape=jax.ShapeDtypeStruct((B, n * D_MODEL), F32),
        compiler_params=_cparams(("arbitrary",)),
        name="ada_mod",
    )(c, w_ada, b_ada.reshape(1, -1))


def _modulated_norm(x, g, scale, shift):
    ms = jnp.mean(x * x, axis=-1, keepdims=True)
    return (x * lax.rsqrt(ms + EPS) * g) * (1.0 + scale) + shift


def _in_proj_kernel(x_ref, mod_ref, g_ref, w_ref, o_ref, h_ref, *hs_ref, r):
    tm = h_ref.shape[0]
    n = tm // r

    @pl.when(pl.program_id(2) == 0)
    def _():
        h = _modulated_norm(x_ref[0], g_ref[...], mod_ref[0, 1:2, :], mod_ref[0, 0:1, :])
        if r == 1:
            h_ref[...] = h.astype(BF16)
        else:
            for c in range(D_MODEL // LANES):
                hs_ref[0][c] = h[:, c * LANES:(c + 1) * LANES]
            for c in range(D_MODEL // LANES):
                h_ref[:, c * LANES:(c + 1) * LANES] = jnp.concatenate(
                    [hs_ref[0][c, pl.ds(rho, n, stride=r), :] for rho in range(r)], axis=0).astype(BF16)

    res = jnp.dot(h_ref[...], w_ref[...], preferred_element_type=F32).astype(BF16)
    if r == 1:
        o_ref[0] = res
    else:
        for rho in range(r):
            o_ref[0, rho] = res[rho * n:(rho + 1) * n, :]


def in_proj(x, mod, g_norm, w_bf16, r=1, tm=2048, tn=None):
    B, S, _ = x.shape
    N = w_bf16.shape[1]
    tn = N if tn is None else tn
    tm = min(tm, S)
    if r == 1:
        out_spec = pl.BlockSpec((1, tm, tn), lambda b, i, j: (b, i, j))
        out_shape = jax.ShapeDtypeStruct((B, S, N), BF16)
        scratch = [pltpu.VMEM((tm, D_MODEL), BF16)]
    else:
        out_spec = pl.BlockSpec((1, r, tm // r, tn), lambda b, i, j: (b, 0, i, j))
        out_shape = jax.ShapeDtypeStruct((B, r, S // r, N), BF16)
        scratch = [pltpu.VMEM((tm, D_MODEL), BF16), pltpu.VMEM((D_MODEL // LANES, tm, LANES), F32)]
    return pl.pallas_call(
        functools.partial(_in_proj_kernel, r=r),
        grid=(B, S // tm, N // tn),
        in_specs=[pl.BlockSpec((1, tm, D_MODEL), lambda b, i, j: (b, i, 0)),
                  pl.BlockSpec((1, 6, D_MODEL), lambda b, i, j: (b, 0, 0)),
                  pl.BlockSpec((1, D_MODEL), lambda b, i, j: (0, 0)),
                  pl.BlockSpec((D_MODEL, tn), lambda b, i, j: (0, j))],
        out_specs=out_spec,
        out_shape=out_shape,
        scratch_shapes=scratch,
        compiler_params=_cparams(("arbitrary", "arbitrary", "arbitrary")),
        name=f"in_proj_r{r}",
    )(x, mod, g_norm.reshape(1, -1), w_bf16)


DA_TQ = 512
DA_TK = 512
DA_D_LO = -(-(BUCKET_CONST_FROM - 1 + DA_TK) // DA_TQ)
DA_D_HI = -(-(BUCKET_CONST_FROM - 1 + DA_TQ) // DA_TQ)
DA_NBIAS = DA_D_LO + DA_D_HI + 1
DA_VALUE_PARTS = 1
DA_VROWS = 144
LOG2E = 1.4426950408889634


def _da_kernel(tab_ref, q_ref, k_ref, v_ref, lq1_ref, lk1_ref, lq2_ref, lk2_ref, gsub_ref, o_ref,
               bias_ref, vt_ref, s_ref, p_ref, *, n_chunks):
    TQ, TK = DA_TQ, DA_TK
    head = pl.program_id(0)
    qi = pl.program_id(2)

    @pl.when((pl.program_id(1) == 0) & (qi == 0))
    def _():
        row = lax.broadcasted_iota(jnp.int32, (TK, TQ), 0)
        col = lax.broadcasted_iota(jnp.int32, (TK, TQ), 1)
        for d in range(DA_NBIAS):
            bias_ref[d] = _bias_of_rel((d - DA_D_LO) * TQ + row - col, tab_ref, head, DA_HEADS) * LOG2E

    @pl.when(qi == 0)
    def _():
        extra = lax.broadcasted_iota(jnp.int32, (DA_VROWS - LANES, TK), 0)
        ones_row = jnp.where(extra == 0, 1.0, 0.0).astype(BF16)
        for c in range(n_chunks):
            vt_ref[0:LANES, c * TK:(c + 1) * TK] = v_ref[0, c * TK:(c + 1) * TK, :].astype(F32).T.astype(BF16)
            vt_ref[LANES:DA_VROWS, c * TK:(c + 1) * TK] = ones_row

    q = q_ref[0].astype(F32) * (DA_HEAD_DIM ** -0.5 * LOG2E)
    lane = lax.broadcasted_iota(jnp.int32, q.shape, 1)
    q1 = jnp.where(lane < DA_HEAD_DIM, q, 0.0).astype(BF16)
    q2 = jnp.where(lane >= DA_HEAD_DIM, q, 0.0).astype(BF16)

    def score_chunk(kc, carry):
        m1, m2 = carry
        k = k_ref[0, pl.ds(pl.multiple_of(kc * TK, TK), TK), :]
        d = kc * (TK // TQ) - qi
        bias = bias_ref[jnp.clip(d, -DA_D_LO, DA_D_HI) + DA_D_LO]
        s1 = _dot_nt(k, q1) + bias
        s2 = _dot_nt(k, q2) + bias
        s_ref[0, kc] = s1
        s_ref[1, kc] = s2
        return (jnp.maximum(m1, jnp.max(s1, axis=0, keepdims=True)),
                jnp.maximum(m2, jnp.max(s2, axis=0, keepdims=True)))

    neg = jnp.full((1, TQ), -jnp.inf, F32)
    m1, m2 = lax.fori_loop(0, n_chunks, score_chunk, (neg, neg), unroll=4)

    per_part = n_chunks // DA_VALUE_PARTS
    a1 = a2 = None
    for part in range(DA_VALUE_PARTS):
        for c in range(part * per_part, (part + 1) * per_part):
            p_ref[0, c * TK:(c + 1) * TK, :] = jnp.exp2((s_ref[0, c] - m1).astype(BF16))
            p_ref[1, c * TK:(c + 1) * TK, :] = jnp.exp2((s_ref[1, c] - m2).astype(BF16))
        keys = slice(part * per_part * TK, (part + 1) * per_part * TK)
        d1 = jnp.dot(vt_ref[:, keys], p_ref[0, keys, :], preferred_element_type=F32)
        d2 = jnp.dot(vt_ref[:, keys], p_ref[1, keys, :], preferred_element_type=F32)
        a1 = d1 if a1 is None else a1 + d1
        a2 = d2 if a2 is None else a2 + d2

    lam = (jnp.exp(jnp.sum(lq1_ref[...] * lk1_ref[...], axis=-1, keepdims=True))
           - jnp.exp(jnp.sum(lq2_ref[...] * lk2_ref[...], axis=-1, keepdims=True)) + LAM_INIT)
    ot = a1[0:LANES] / a1[LANES:LANES + 1] - lam * (a2[0:LANES] / a2[LANES:LANES + 1])
    ms = jnp.mean(ot * ot, axis=0, keepdims=True)
    y = ot * lax.rsqrt(ms + EPS) * gsub_ref[...] * (1.0 - LAM_INIT)
    o_ref[0] = y.T.astype(BF16)


def diff_attn(p, tab, lq1, lk1, lq2, lk2, g_subln):
    B, S, _ = p.shape
    TQ, TK = DA_TQ, DA_TK
    n_chunks = S // TK
    small = pl.BlockSpec((1, DA_HEAD_DIM), lambda h, b, i: (0, 0))
    return pl.pallas_call(
        functools.partial(_da_kernel, n_chunks=n_chunks),
        grid=(DA_HEADS, B, S // TQ),
        in_specs=[pl.BlockSpec(memory_space=pltpu.SMEM),
                  pl.BlockSpec((1, TQ, LANES), lambda h, b, i: (b, i, OFF_QA // LANES + h)),
                  pl.BlockSpec((1, S, LANES), lambda h, b, i: (b, 0, OFF_KA // LANES + h)),
                  pl.BlockSpec((1, S, LANES), lambda h, b, i: (b, 0, OFF_VA // LANES + h)),
                  small, small, small, small,
                  pl.BlockSpec((2 * DA_HEAD_DIM, 1), lambda h, b, i: (0, 0))],
        out_specs=pl.BlockSpec((1, TQ, LANES), lambda h, b, i: (b, i, h)),
        out_shape=jax.ShapeDtypeStruct((B, S, DA_V), BF16),
        scratch_shapes=[pltpu.VMEM((DA_NBIAS, TK, TQ), F32),
                        pltpu.VMEM((DA_VROWS, S), BF16),
                        pltpu.VMEM((2, n_chunks, TK, TQ), F32),
                        pltpu.VMEM((2, S, TQ), BF16)],
        compiler_params=_cparams(("arbitrary", "arbitrary", "arbitrary")),
        name="diff_attn",
    )(tab, p, p, p, lq1.reshape(1, -1), lk1.reshape(1, -1), lq2.reshape(1, -1), lk2.reshape(1, -1),
      g_subln.reshape(-1, 1))


DIL_TQ = 128
DIL_PAD = DIL_HALF_SPAN
DIL_WIN = DIL_TQ + 2 * DIL_PAD


def _dil_kernel(tab_ref, q_ref, k_ref, v_ref, o_ref, lse_ref, kpad_ref, vpad_ref, mask_ref, *, L, LQ, dilation):
    TQ, W, hs = DIL_TQ, DIL_WIN, DIL_HALF_SPAN
    n_tiles = L // TQ
    qblock = pl.program_id(2)

    @pl.when((pl.program_id(0) == 0) & (pl.program_id(1) == 0) & (qblock == 0))
    def _():
        row = lax.broadcasted_iota(jnp.int32, (TQ, W), 0)
        col = lax.broadcasted_iota(jnp.int32, (TQ, W), 1)
        rel = col - DIL_PAD - row
        band = jnp.abs(rel) <= hs
        for variant, qs in enumerate((0, TQ, L - TQ)):
            kpos = qs - DIL_PAD + col
            ok = band & (kpos >= 0) & (kpos < L)
            for h in range(DIL_HEADS):
                mask_ref[variant, h] = jnp.where(ok, _bias_of_rel(dilation * rel, tab_ref, h, DIL_HEADS), NEG_BIG)

    @pl.when(qblock == 0)
    def _():
        pad = jnp.zeros((DIL_PAD, DIL_COLS), BF16)
        for ref, src in ((kpad_ref, k_ref), (vpad_ref, v_ref)):
            ref[0:DIL_PAD, :] = pad
            ref[DIL_PAD + L:2 * DIL_PAD + L, :] = pad
            ref[DIL_PAD:DIL_PAD + L, :] = src[0]

    lane = lax.broadcasted_iota(jnp.int32, (TQ, LANES), 1)
    low = lane < DIL_HEAD_DIM

    def tile(t, _):
        qs = pl.multiple_of(t * TQ, TQ)
        tg = qblock * (LQ // TQ) + t
        ks = pl.multiple_of(tg * TQ, TQ)
        variant = jnp.where(tg == 0, 0, jnp.where(tg == n_tiles - 1, 2, 1))
        lse_tile = jnp.zeros((TQ, LANES), F32)
        scores = []
        for h in range(DIL_HEADS):
            cols = slice(h // 2 * LANES, (h // 2 + 1) * LANES)
            q = q_ref[0, pl.ds(qs, TQ), cols].astype(F32) * (DIL_HEAD_DIM ** -0.5)
            qm = jnp.where(low if h % 2 == 0 else ~low, q, 0.0).astype(BF16)
            scores.append(_dot_nt(qm, kpad_ref[pl.ds(ks, W), cols]) + mask_ref[variant, h])
        probs, dens = [], []
        for h in range(DIL_HEADS):
            m = jnp.max(scores[h], axis=-1, keepdims=True)
            pr = jnp.exp(scores[h] - m)
            den = jnp.sum(pr, axis=-1, keepdims=True)
            probs.append(pr.astype(BF16))
            dens.append(den)
            lse_tile = jnp.where(lane == h, m + jnp.log(den), lse_tile)
        for hp in range(DIL_HEADS // 2):
            cols = slice(hp * LANES, (hp + 1) * LANES)
            vw = vpad_ref[pl.ds(ks, W), cols]
            outs = [jnp.dot(probs[2 * hp + e], vw, preferred_element_type=F32) / dens[2 * hp + e] for e in range(2)]
            o_ref[0, pl.ds(qs, TQ), cols] = jnp.where(low, outs[0], outs[1])
        lse_ref[0, pl.ds(qs, TQ), :] = lse_tile
        return 0

    lax.fori_loop(0, LQ // TQ, tile, 0)


def dilated_group(pg, tab, qkv_col, dilation, lq=2048):
    B, r, L, _ = pg.shape
    assert r == dilation and L >= 2 * DIL_TQ and L % DIL_TQ == 0
    LQ = min(lq, L)

    def col(which, rows, tiled):
        cb = qkv_col // DIL_COLS + which
        return pl.BlockSpec((1, None, rows, DIL_COLS), lambda b, rho, t: (b, rho, t if tiled else 0, cb))

    return pl.pallas_call(
        functools.partial(_dil_kernel, L=L, LQ=LQ, dilation=r),
        grid=(B, r, L // LQ),
        in_specs=[pl.BlockSpec(memory_space=pltpu.SMEM),
                  col(0, LQ, True), col(1, L, False), col(2, L, False)],
        out_specs=[pl.BlockSpec((1, None, LQ, DIL_COLS), lambda b, rho, t: (b, rho, t, 0)),
                   pl.BlockSpec((1, None, LQ, LANES), lambda b, rho, t: (b, rho, t, 0))],
        out_shape=[jax.ShapeDtypeStruct((B, r, L, DIL_COLS), F32),
                   jax.ShapeDtypeStruct((B, r, L, LANES), F32)],
        scratch_shapes=[pltpu.VMEM((L + 2 * DIL_PAD, DIL_COLS), BF16),
                        pltpu.VMEM((L + 2 * DIL_PAD, DIL_COLS), BF16),
                        pltpu.VMEM((3, DIL_HEADS, DIL_TQ, DIL_WIN), F32)],
        compiler_params=_cparams(("arbitrary", "arbitrary", "arbitrary")),
        name=f"dilated_r{r}",
    )(tab, pg, pg, pg)


def _split_dot(w, e_ref):
    hi = w.astype(BF16)
    lo = (w - hi.astype(F32)).astype(BF16)
    e = e_ref[...]
    return jnp.dot(hi, e, preferred_element_type=F32) + jnp.dot(lo, e, preferred_element_type=F32)


def _token_order(src_ref, tmp_ref, r):
    if r == 1:
        return src_ref[0, 0]
    n, width = src_ref.shape[2], src_ref.shape[3]
    for c in range(width // LANES):
        for rho in range(r):
            tmp_ref[c, pl.ds(rho, n, stride=r), :] = src_ref[0, rho, :, c * LANES:(c + 1) * LANES]
    return jnp.concatenate([tmp_ref[c] for c in range(width // LANES)], axis=1)


def _merge_kernel(x_ref, mod_ref, oa_ref, o0_ref, o1_ref, o2_ref, l0_ref, l1_ref, l2_ref,
                  ga0_ref, ga1_ref, gb0_ref, gb1_ref, e_ref, wpa_ref, wpb_ref, wout_ref, gffn_ref,
                  x1_ref, h2_ref, to1_ref, to2_ref, tl1_ref, tl2_ref):
    dil = [d for _, d in DIL_GROUPS]
    a = jnp.dot(oa_ref[0], wpa_ref[...], preferred_element_type=F32)
    l0 = _token_order(l0_ref, None, dil[0])
    l1 = _token_order(l1_ref, tl1_ref, dil[1])
    l2 = _token_order(l2_ref, tl2_ref, dil[2])
    mx = jnp.maximum(jnp.maximum(l0, l1), l2)
    e0, e1, e2 = jnp.exp(l0 - mx), jnp.exp(l1 - mx), jnp.exp(l2 - mx)
    den = e0 + e1 + e2
    ob = (_token_order(o0_ref, None, dil[0]) * _split_dot(e0 / den, e_ref)
          + _token_order(o1_ref, to1_ref, dil[1]) * _split_dot(e1 / den, e_ref)
          + _token_order(o2_ref, to2_ref, dil[2]) * _split_dot(e2 / den, e_ref))
    b = jnp.dot(ob.astype(BF16), wpb_ref[...], preferred_element_type=F32)
    ga = jnp.concatenate([ga0_ref[0], ga1_ref[0]], axis=-1).astype(F32)
    gb = jnp.concatenate([gb0_ref[0], gb1_ref[0]], axis=-1).astype(F32)
    merged = a / (1.0 + jnp.exp(-ga)) + b / (1.0 + jnp.exp(-gb))
    mix = jnp.dot(merged.astype(BF16), wout_ref[...], preferred_element_type=F32)
    x1 = x_ref[0] + mod_ref[0, 2:3, :] * mix
    x1_ref[0] = x1
    h2_ref[0] = _modulated_norm(x1, gffn_ref[...], mod_ref[0, 4:5, :], mod_ref[0, 3:4, :]).astype(BF16)


def merge_proj(x, mod, p_nat, oa, outs, lses, wpa, wpb, wout, g_ffn, tm=512):
    B, S, _ = x.shape
    tm = min(tm, S)
    expand = np.zeros((LANES, DIL_COLS), np.float32)
    for h in range(DIL_HEADS):
        expand[h, h * DIL_HEAD_DIM:(h + 1) * DIL_HEAD_DIM] = 1.0
    expand = jnp.asarray(expand, BF16)

    def tok(w):
        return pl.BlockSpec((1, tm, w), lambda b, i: (b, i, 0))

    def res(w, r):
        return pl.BlockSpec((1, r, tm // r, w), lambda b, i: (b, 0, i, 0))

    def pcol(off):
        return pl.BlockSpec((1, tm, DIL_COLS), lambda b, i: (b, i, off // DIL_COLS))

    def whole(a):
        return pl.BlockSpec(a.shape, lambda b, i: (0,) * a.ndim)

    g = g_ffn.reshape(1, -1)
    dil = [d for _, d in DIL_GROUPS]
    return pl.pallas_call(
        _merge_kernel,
        grid=(B, S // tm),
        in_specs=[tok(D_MODEL), pl.BlockSpec((1, 6, D_MODEL), lambda b, i: (b, 0, 0)), tok(DA_V),
                  res(DIL_COLS, dil[0]), res(DIL_COLS, dil[1]), res(DIL_COLS, dil[2]),
                  res(LANES, dil[0]), res(LANES, dil[1]), res(LANES, dil[2]),
                  pcol(NAT_GA), pcol(NAT_GA + DIL_COLS), pcol(NAT_GB), pcol(NAT_GB + DIL_COLS),
                  whole(expand), whole(wpa), whole(wpb), whole(wout), whole(g)],
        out_specs=[tok(D_MODEL), tok(D_MODEL)],
        out_shape=[jax.ShapeDtypeStruct((B, S, D_MODEL), F32), jax.ShapeDtypeStruct((B, S, D_MODEL), BF16)],
        scratch_shapes=[pltpu.VMEM((DIL_COLS // LANES, tm, LANES), F32),
                        pltpu.VMEM((DIL_COLS // LANES, tm, LANES), F32),
                        pltpu.VMEM((1, tm, LANES), F32), pltpu.VMEM((1, tm, LANES), F32)],
        compiler_params=_cparams(("arbitrary", "arbitrary")),
        name="merge_proj",
    )(x, mod, oa, outs[0], outs[1], outs[2], lses[0], lses[1], lses[2], p_nat, p_nat, p_nat, p_nat,
      expand, wpa, wpb, wout, g)


def _top_values(s, count, ranks=False):
    row = lax.broadcasted_iota(jnp.int32, (count, s.shape[1]), 0)
    rows = []
    top = jnp.zeros((count, s.shape[1]), F32)
    rank = jnp.full(s.shape, float(count), F32)
    for i in range(count):
        m = jnp.max(s, axis=0, keepdims=True)
        rows.append(m)
        top = jnp.where(row == i, m, top)
        hit = s >= m
        if ranks:
            rank = jnp.where(hit, float(i), rank)
        s = jnp.where(hit, -jnp.inf, s)
    return (rows, top, rank) if ranks else (rows, top)


def _paired_bf16(x):
    bits = pltpu.bitcast(x.astype(BF16).astype(F32), jnp.uint32)
    return bits | (bits >> 16)


def _route_kernel(h_ref, wq_ref, k1_ref, k2_ref, r2_ref, e2_ref, cnt_ref, e1_ref):
    qp = jnp.dot(h_ref[...], wq_ref[...], preferred_element_type=F32)
    tm = qp.shape[0]
    K, half = PEER_TOPK, PEER_TOPK // 2
    first_row = lax.broadcasted_iota(jnp.int32, (K, tm), 0) == 0
    for h in range(PEER_HEADS):
        q1 = qp[:, h * PEER_QUERY_DIM:h * PEER_QUERY_DIM + PEER_HALF].astype(BF16)
        q2 = qp[:, h * PEER_QUERY_DIM + PEER_HALF:(h + 1) * PEER_QUERY_DIM].astype(BF16)
        s1 = _dot_nt(k1_ref[h], q1)
        s2 = _dot_nt(k2_ref[h], q2)
        r1, v1 = _top_values(s1, K)
        r2, v2, rank2 = _top_values(s2, K, ranks=True)
        pieces = [v1 + r2[0]] + [v1[0:half] + r2[j] for j in range(1, half)] + [v2[half:K] + r1[0]]
        cand = jnp.concatenate(pieces, axis=0)
        tau = _top_values(cand, K)[0][K - 1]
        top = r1[0] + r2[0]
        z = jnp.sum(jnp.where(cand >= tau, jnp.exp(cand - top), 0.0), axis=0, keepdims=True)
        reach = [jnp.where(p >= tau, 1.0, 0.0) for p in pieces]
        low = reach[1]
        for p in reach[2:half]:
            low = low + p
        tail = jnp.sum(reach[half], axis=0, keepdims=True)
        cnt_rank = reach[0] + jnp.concatenate([low, jnp.zeros_like(low)], axis=0) + jnp.where(first_row, tail, 0.0)
        cnt = jnp.zeros(s1.shape, F32)
        for i in range(K):
            cnt = jnp.where(s1 == r1[i], cnt_rank[i:i + 1], cnt)
        r2_ref[h] = rank2.astype(BF16)
        e2_ref[h] = (jnp.exp(s2 - r2[0]) / z).astype(BF16)
        cnt_ref[h] = _paired_bf16(cnt)
        e1_ref[h] = _paired_bf16(0.5 * jnp.exp(s1 - r1[0]))


def peer_route(h2, wq, k1, k2, tm=256):
    T = h2.shape[0]
    tm = min(tm, T)
    half = jax.ShapeDtypeStruct((PEER_HEADS, N_KEYS, T), BF16)
    word = jax.ShapeDtypeStruct((PEER_HEADS, N_KEYS, T), jnp.uint32)
    bspec = pl.BlockSpec((PEER_HEADS, N_KEYS, tm), lambda i: (0, 0, i))
    kspec = pl.BlockSpec((PEER_HEADS, N_KEYS, PEER_HALF), lambda i: (0, 0, 0))
    return pl.pallas_call(
        _route_kernel,
        grid=(T // tm,),
        in_specs=[pl.BlockSpec((tm, D_MODEL), lambda i: (i, 0)),
                  pl.BlockSpec(wq.shape, lambda i: (0, 0)), kspec, kspec],
        out_specs=[bspec, bspec, bspec, bspec],
        out_shape=[half, half, word, word],
        compiler_params=_cparams(("arbitrary",)),
        name="peer_route",
    )(h2, wq, k1, k2)


PEER_TE = 2048
PEER_SLAB_GROUP = 8
PACK = 16


def _dense_kernel(h_ref, u_ref, vt_ref, r2_ref, e2_ref, cnt_ref, e1_ref, x1_ref, mod_ref, gfin_ref,
                  y_ref, acc_ref, a_ref):
    j = pl.program_id(1)
    tm = h_ref.shape[0]

    @pl.when(j == 0)
    def _():
        acc_ref[...] = jnp.zeros_like(acc_ref)

    def rows_bf16(ref, h, first):
        block = ref[h, pl.ds(first, PEER_SLAB_GROUP), :]
        return [pltpu.bitcast(jnp.broadcast_to(block[al:al + 1, :], (8, tm)), BF16) for al in range(PEER_SLAB_GROUP)]

    zero = jnp.zeros((PACK, tm), BF16)
    group_rows = PEER_SLAB_GROUP * N_KEYS
    n_groups = PEER_TE // group_rows
    zts = [_dot_nt(u_ref[g * group_rows:(g + 1) * group_rows, :], h_ref[...]) for g in range(n_groups)]
    for sg in range(n_groups):
        experts = slice(sg * group_rows, (sg + 1) * group_rows)
        zt = zts[sg]
        first = pl.multiple_of(j * (PEER_TE // N_KEYS) + sg * PEER_SLAB_GROUP, PEER_SLAB_GROUP)
        cnt_rows = [rows_bf16(cnt_ref, h, first) for h in range(PEER_HEADS)]
        e1_rows = [rows_bf16(e1_ref, h, first) for h in range(PEER_HEADS)]
        for c in range(N_KEYS // PACK):
            keys = slice(c * PACK, (c + 1) * PACK)
            w = [None] * PEER_SLAB_GROUP
            for h in range(PEER_HEADS):
                r2 = r2_ref[h, keys, :]
                e2 = e2_ref[h, keys, :]
                for al in range(PEER_SLAB_GROUP):
                    term = jnp.where(r2 < cnt_rows[h][al], e2, zero) * e1_rows[h][al]
                    w[al] = term if w[al] is None else w[al] + term
            for al in range(PEER_SLAB_GROUP):
                base = al * N_KEYS + c * PACK
                z = zt[base:base + PACK, :]
                a_ref[sg * group_rows + base:sg * group_rows + base + PACK, :] = (
                    (z * (1.0 + lax.erf(z * (2.0 ** -0.5)))).astype(BF16) * w[al])
        acc_ref[...] += jnp.dot(vt_ref[:, experts], a_ref[experts, :], preferred_element_type=F32)

    @pl.when(j == pl.num_programs(1) - 1)
    def _():
        x2 = x1_ref[...] + mod_ref[0, 5:6, :] * acc_ref[...].T
        ms = jnp.mean(x2 * x2, axis=-1, keepdims=True)
        y_ref[...] = x2 * lax.rsqrt(ms + EPS) * gfin_ref[...]


def peer_dense(h2, u_bf16, vt_bf16, routes, x1, mod, g_final, S, tm=512):
    T = h2.shape[0]
    tm = min(tm, S)
    r2, e2, cnt, e1 = routes
    rspec = pl.BlockSpec((PEER_HEADS, N_KEYS, tm), lambda i, j: (0, 0, i))
    tiles_per_row = S // tm
    return pl.pallas_call(
        _dense_kernel,
        grid=(T // tm, N_EXPERTS // PEER_TE),
        in_specs=[pl.BlockSpec((tm, D_MODEL), lambda i, j: (i, 0)),
                  pl.BlockSpec((PEER_TE, D_MODEL), lambda i, j: (j, 0)),
                  pl.BlockSpec((D_MODEL, PEER_TE), lambda i, j: (0, j)),
                  rspec, rspec, rspec, rspec,
                  pl.BlockSpec((tm, D_MODEL), lambda i, j: (i, 0)),
                  pl.BlockSpec((1, 6, D_MODEL), lambda i, j: (i // tiles_per_row, 0, 0)),
                  pl.BlockSpec((1, D_MODEL), lambda i, j: (0, 0))],
        out_specs=pl.BlockSpec((tm, D_MODEL), lambda i, j: (i, 0)),
        out_shape=jax.ShapeDtypeStruct((T, D_MODEL), F32),
        scratch_shapes=[pltpu.VMEM((D_MODEL, tm), F32), pltpu.VMEM((PEER_TE, tm), BF16)],
        compiler_params=_cparams(("arbitrary", "arbitrary")),
        name="peer_dense",
    )(h2, u_bf16, vt_bf16, r2, e2, cnt, e1, x1, mod, g_final.reshape(1, -1))


def _prepare_weights(w_in, w_proj_a, w_proj_b, w_out, rel_bias, w_query, sub_keys_1, sub_keys_2,
                     expert_u, expert_v):
    tabs = [rel_bias[:, :DA_HEADS].reshape(-1)]
    for gi in range(len(DIL_GROUPS)):
        lo = DA_HEADS + gi * DIL_HEADS
        tabs.append(rel_bias[:, lo:lo + DIL_HEADS].reshape(-1))
    w = w_in[0].astype(BF16)

    def group_cols(gi):
        return [w[:, off + gi * DIL_COLS:off + (gi + 1) * DIL_COLS] for off in (OFF_QB, OFF_KB, OFF_VB)]

    w_nat = jnp.concatenate([w[:, :OFF_QB]] + group_cols(0) + [w[:, OFF_GA:]], axis=1)
    w_groups = [jnp.concatenate(group_cols(gi), axis=1) for gi in range(1, len(DIL_GROUPS))]
    return dict(
        w_nat=w_nat, w_groups=w_groups, wpa=w_proj_a[0].astype(BF16), wpb=w_proj_b[0].astype(BF16),
        wout=w_out[0].astype(BF16), tabs=tabs, wq=w_query[0].astype(BF16),
        k1=sub_keys_1[0].astype(BF16), k2=sub_keys_2[0].astype(BF16),
        u=expert_u[0].astype(BF16), vt=expert_v[0].astype(BF16).T)


def _encoder(x, c, w_ada, b_ada, g_norm_mix, g_norm_ffn, lq1, lk1, lq2, lk2, g_subln, g_final, wts):
    B, S, _ = x.shape
    mod = ada_mod(c, w_ada[0], b_ada[0]).reshape(B, 6, D_MODEL)
    p_nat = in_proj(x, mod, g_norm_mix[0], wts["w_nat"], tn=NAT_COLS // 4)
    oa = diff_attn(p_nat, wts["tabs"][0], lq1[0], lk1[0], lq2[0], lk2[0], g_subln[0])
    outs, lses = [], []
    for gi, (_, dilation) in enumerate(DIL_GROUPS):
        if gi == 0:
            pg, qkv_col = p_nat.reshape(B, 1, S, NAT_COLS), NAT_Q0
        else:
            pg, qkv_col = in_proj(x, mod, g_norm_mix[0], wts["w_groups"][gi - 1], r=dilation), 0
        o_g, lse_g = dilated_group(pg, wts["tabs"][1 + gi], qkv_col, dilation)
        outs.append(o_g)
        lses.append(lse_g)
    x1, h2 = merge_proj(x, mod, p_nat, oa, outs, lses, wts["wpa"], wts["wpb"], wts["wout"], g_norm_ffn[0])
    h2 = h2.reshape(B * S, D_MODEL)
    routes = peer_route(h2, wts["wq"], wts["k1"], wts["k2"])
    y = peer_dense(h2, wts["u"], wts["vt"], routes, x1.reshape(B * S, D_MODEL), mod, g_final, S)
    return y.reshape(B, S, D_MODEL)


def kernel(x_prompt, x_sample, c_prompt, c_sample, w_ada, b_ada, g_norm_mix, g_norm_ffn, w_in, w_proj_a, w_proj_b, w_out, lambda_q1, lambda_k1, lambda_q2, lambda_k2, g_subln, rel_bias, w_query, sub_keys_1, sub_keys_2, expert_u, expert_v, g_final):
    wts = _prepare_weights(w_in, w_proj_a, w_proj_b, w_out, rel_bias, w_query, sub_keys_1, sub_keys_2,
                           expert_u, expert_v)
    args = (w_ada, b_ada, g_norm_mix, g_norm_ffn, lambda_q1, lambda_k1, lambda_q2, lambda_k2, g_subln, g_final, wts)
    return _encoder(x_prompt, c_prompt, *args), _encoder(x_sample, c_sample, *args)
```

```python
import functools
import math

import jax
import jax.numpy as jnp
import numpy as np
from jax import lax
from jax.experimental import pallas as pl
from jax.experimental.pallas import tpu as pltpu

F32 = jnp.float32
BF16 = jnp.bfloat16

D_MODEL = 1024
DA_HEADS = 8
DA_HEAD_DIM = 64
DA_QK = DA_HEADS * 2 * DA_HEAD_DIM
DA_V = DA_HEADS * 2 * DA_HEAD_DIM
DIL_GROUPS = ((128, 1), (512, 4), (2048, 16))
DIL_HEADS = 8
DIL_HEAD_DIM = 64
DIL_HALF_SPAN = 64
DIL_COLS = DIL_HEADS * DIL_HEAD_DIM
DIL_QKV = len(DIL_GROUPS) * DIL_COLS
IN_COLS = 2 * DA_QK + DA_V + 3 * DIL_QKV + 2 * D_MODEL
NUM_BUCKETS = 32
N_REL_HEADS = DA_HEADS + len(DIL_GROUPS) * DIL_HEADS
PEER_HEADS = 8
N_KEYS = 128
N_EXPERTS = N_KEYS * N_KEYS
PEER_QUERY_DIM = 256
PEER_HALF = PEER_QUERY_DIM // 2
PEER_TOPK = 16
EPS = 1e-6
NEG_BIG = -1e30
LAM_INIT = 0.8 - 0.6 * math.exp(-0.3 * 0)

LANES = 128
VMEM_LIMIT = 56 * 1024 * 1024

OFF_QA, OFF_KA, OFF_VA = 0, DA_QK, 2 * DA_QK
OFF_QB = 2 * DA_QK + DA_V
OFF_KB = OFF_QB + DIL_QKV
OFF_VB = OFF_KB + DIL_QKV
OFF_GA = OFF_VB + DIL_QKV
OFF_GB = OFF_GA + D_MODEL
NAT_Q0 = OFF_QB
NAT_GA = NAT_Q0 + 3 * DIL_COLS
NAT_GB = NAT_GA + D_MODEL
NAT_COLS = NAT_GB + D_MODEL

BUCKET_THRESHOLDS = (1, 2, 3, 4, 5, 6, 7, 8, 15, 27, 50, 91, 166, 305, 559)
BUCKET_CONST_FROM = BUCKET_THRESHOLDS[-1]


def _cparams(sem):
    return pltpu.CompilerParams(dimension_semantics=sem, vmem_limit_bytes=VMEM_LIMIT)


def _dot_nt(a, b):
    return lax.dot_general(a, b, (((1,), (1,)), ((), ())), preferred_element_type=F32)


def _bias_of_rel(rel, tab_ref, head, n_heads):
    n = jnp.abs(rel)
    neg = jnp.full(rel.shape, tab_ref[0 * n_heads + head], F32)
    pos = jnp.full(rel.shape, tab_ref[16 * n_heads + head], F32)
    for k, thr in enumerate(BUCKET_THRESHOLDS, start=1):
        ge = n >= thr
        neg = jnp.where(ge, tab_ref[k * n_heads + head], neg)
        pos = jnp.where(ge, tab_ref[(16 + k) * n_heads + head], pos)
    return jnp.where(rel > 0, pos, neg)


def _ada_kernel(c_ref, w_ref, b_ref, o_ref):
    c = c_ref[...]
    sc = c / (1.0 + jnp.exp(-c))
    o_ref[...] = jnp.dot(sc, w_ref[...], preferred_element_type=F32,
                         precision=lax.Precision.HIGHEST) + b_ref[...]


def ada_mod(c, w_ada, b_ada):
    B = c.shape[0]
    n = w_ada.shape[1] // D_MODEL
    return pl.pallas_call(
        _ada_kernel,
        grid=(n,),
        in_specs=[pl.BlockSpec((B, D_MODEL), lambda j: (0, 0)),
                  pl.BlockSpec((D_MODEL, D_MODEL), lambda j: (0, j)),
                  pl.BlockSpec((1, D_MODEL), lambda j: (0, j))],
        out_specs=pl.BlockSpec((B, D_MODEL), lambda j: (0, j)),
        out_shape=jax.ShapeDtypeStruct((B, n * D_MODEL), F32),
        compiler_params=_cparams(("arbitrary",)),
        name="ada_mod",
    )(c, w_ada, b_ada.reshape(1, -1))


def _modulated_norm(x, g, scale, shift):
    ms = jnp.mean(x * x, axis=-1, keepdims=True)
    return (x * lax.rsqrt(ms + EPS) * g) * (1.0 + scale) + shift


def _in_proj_kernel(x_ref, mod_ref, g_ref, w_ref, o_ref, h_ref, *hs_ref, r):
    tm = h_ref.shape[0]
    n = tm // r

    @pl.when(pl.program_id(2) == 0)
    def _():
        h = _modulated_norm(x_ref[0], g_ref[...], mod_ref[0, 1:2, :], mod_ref[0, 0:1, :])
        if r == 1:
            h_ref[...] = h.astype(BF16)
        else:
            for c in range(D_MODEL // LANES):
                hs_ref[0][c] = h[:, c * LANES:(c + 1) * LANES]
            for c in range(D_MODEL // LANES):
                h_ref[:, c * LANES:(c + 1) * LANES] = jnp.concatenate(
                    [hs_ref[0][c, pl.ds(rho, n, stride=r), :] for rho in range(r)], axis=0).astype(BF16)

    res = jnp.dot(h_ref[...], w_ref[...], preferred_element_type=F32).astype(BF16)
    if r == 1:
        o_ref[0] = res
    else:
        for rho in range(r):
            o_ref[0, rho] = res[rho * n:(rho + 1) * n, :]


def in_proj(x, mod, g_norm, w_bf16, r=1, tm=2048, tn=None):
    B, S, _ = x.shape
    N = w_bf16.shape[1]
    tn = N if tn is None else tn
    tm = min(tm, S)
    if r == 1:
        out_spec = pl.BlockSpec((1, tm, tn), lambda b, i, j: (b, i, j))
        out_shape = jax.ShapeDtypeStruct((B, S, N), BF16)
        scratch = [pltpu.VMEM((tm, D_MODEL), BF16)]
    else:
        out_spec = pl.BlockSpec((1, r, tm // r, tn), lambda b, i, j: (b, 0, i, j))
        out_shape = jax.ShapeDtypeStruct((B, r, S // r, N), BF16)
        scratch = [pltpu.VMEM((tm, D_MODEL), BF16), pltpu.VMEM((D_MODEL // LANES, tm, LANES), F32)]
    return pl.pallas_call(
        functools.partial(_in_proj_kernel, r=r),
        grid=(B, S // tm, N // tn),
        in_specs=[pl.BlockSpec((1, tm, D_MODEL), lambda b, i, j: (b, i, 0)),
                  pl.BlockSpec((1, 6, D_MODEL), lambda b, i, j: (b, 0, 0)),
                  pl.BlockSpec((1, D_MODEL), lambda b, i, j: (0, 0)),
                  pl.BlockSpec((D_MODEL, tn), lambda b, i, j: (0, j))],
        out_specs=out_spec,
        out_shape=out_shape,
        scratch_shapes=scratch,
        compiler_params=_cparams(("arbitrary", "arbitrary", "arbitrary")),
        name=f"in_proj_r{r}",
    )(x, mod, g_norm.reshape(1, -1), w_bf16)


DA_TQ = 512
DA_TK = 512
DA_D_LO = -(-(BUCKET_CONST_FROM - 1 + DA_TK) // DA_TQ)
DA_D_HI = -(-(BUCKET_CONST_FROM - 1 + DA_TQ) // DA_TQ)
DA_NBIAS = DA_D_LO + DA_D_HI + 1
DA_VALUE_PARTS = 1
DA_VROWS = 144
LOG2E = 1.4426950408889634


def _da_kernel(tab_ref, q_ref, k_ref, v_ref, lq1_ref, lk1_ref, lq2_ref, lk2_ref, gsub_ref, o_ref,
               bias_ref, vt_ref, s_ref, p_ref, *, n_chunks):
    TQ, TK = DA_TQ, DA_TK
    head = pl.program_id(0)
    qi = pl.program_id(2)

    @pl.when((pl.program_id(1) == 0) & (qi == 0))
    def _():
        row = lax.broadcasted_iota(jnp.int32, (TK, TQ), 0)
        col = lax.broadcasted_iota(jnp.int32, (TK, TQ), 1)
        for d in range(DA_NBIAS):
            bias_ref[d] = _bias_of_rel((d - DA_D_LO) * TQ + row - col, tab_ref, head, DA_HEADS) * LOG2E

    @pl.when(qi == 0)
    def _():
        extra = lax.broadcasted_iota(jnp.int32, (DA_VROWS - LANES, TK), 0)
        ones_row = jnp.where(extra == 0, 1.0, 0.0).astype(BF16)
        for c in range(n_chunks):
            vt_ref[0:LANES, c * TK:(c + 1) * TK] = v_ref[0, c * TK:(c + 1) * TK, :].astype(F32).T.astype(BF16)
            vt_ref[LANES:DA_VROWS, c * TK:(c + 1) * TK] = ones_row

    q = q_ref[0].astype(F32) * (DA_HEAD_DIM ** -0.5 * LOG2E)
    lane = lax.broadcasted_iota(jnp.int32, q.shape, 1)
    q1 = jnp.where(lane < DA_HEAD_DIM, q, 0.0).astype(BF16)
    q2 = jnp.where(lane >= DA_HEAD_DIM, q, 0.0).astype(BF16)

    def score_chunk(kc, carry):
        m1, m2 = carry
        k = k_ref[0, pl.ds(pl.multiple_of(kc * TK, TK), TK), :]
        d = kc * (TK // TQ) - qi
        bias = bias_ref[jnp.clip(d, -DA_D_LO, DA_D_HI) + DA_D_LO]
        s1 = _dot_nt(k, q1) + bias
        s2 = _dot_nt(k, q2) + bias
        s_ref[0, kc] = s1
        s_ref[1, kc] = s2
        return (jnp.maximum(m1, jnp.max(s1, axis=0, keepdims=True)),
                jnp.maximum(m2, jnp.max(s2, axis=0, keepdims=True)))

    neg = jnp.full((1, TQ), -jnp.inf, F32)
    m1, m2 = lax.fori_loop(0, n_chunks, score_chunk, (neg, neg), unroll=4)

    per_part = n_chunks // DA_VALUE_PARTS
    a1 = a2 = None
    for part in range(DA_VALUE_PARTS):
        for c in range(part * per_part, (part + 1) * per_part):
            p_ref[0, c * TK:(c + 1) * TK, :] = jnp.exp2((s_ref[0, c] - m1).astype(BF16))
            p_ref[1, c * TK:(c + 1) * TK, :] = jnp.exp2((s_ref[1, c] - m2).astype(BF16))
        keys = slice(part * per_part * TK, (part + 1) * per_part * TK)
        d1 = jnp.dot(vt_ref[:, keys], p_ref[0, keys, :], preferred_element_type=F32)
        d2 = jnp.dot(vt_ref[:, keys], p_ref[1, keys, :], preferred_element_type=F32)
        a1 = d1 if a1 is None else a1 + d1
        a2 = d2 if a2 is None else a2 + d2

    lam = (jnp.exp(jnp.sum(lq1_ref[...] * lk1_ref[...], axis=-1, keepdims=True))
           - jnp.exp(jnp.sum(lq2_ref[...] * lk2_ref[...], axis=-1, keepdims=True)) + LAM_INIT)
    ot = a1[0:LANES] / a1[LANES:LANES + 1] - lam * (a2[0:LANES] / a2[LANES:LANES + 1])
    ms = jnp.mean(ot * ot, axis=0, keepdims=True)
    y = ot * lax.rsqrt(ms + EPS) * gsub_ref[...] * (1.0 - LAM_INIT)
    o_ref[0] = y.T.astype(BF16)


def diff_attn(p, tab, lq1, lk1, lq2, lk2, g_subln):
    B, S, _ = p.shape
    TQ, TK = DA_TQ, DA_TK
    n_chunks = S // TK
    small = pl.BlockSpec((1, DA_HEAD_DIM), lambda h, b, i: (0, 0))
    return pl.pallas_call(
        functools.partial(_da_kernel, n_chunks=n_chunks),
        grid=(DA_HEADS, B, S // TQ),
        in_specs=[pl.BlockSpec(memory_space=pltpu.SMEM),
                  pl.BlockSpec((1, TQ, LANES), lambda h, b, i: (b, i, OFF_QA // LANES + h)),
                  pl.BlockSpec((1, S, LANES), lambda h, b, i: (b, 0, OFF_KA // LANES + h)),
                  pl.BlockSpec((1, S, LANES), lambda h, b, i: (b, 0, OFF_VA // LANES + h)),
                  small, small, small, small,
                  pl.BlockSpec((2 * DA_HEAD_DIM, 1), lambda h, b, i: (0, 0))],
        out_specs=pl.BlockSpec((1, TQ, LANES), lambda h, b, i: (b, i, h)),
        out_shape=jax.ShapeDtypeStruct((B, S, DA_V), BF16),
        scratch_shapes=[pltpu.VMEM((DA_NBIAS, TK, TQ), F32),
                        pltpu.VMEM((DA_VROWS, S), BF16),
                        pltpu.VMEM((2, n_chunks, TK, TQ), F32),
                        pltpu.VMEM((2, S, TQ), BF16)],
        compiler_params=_cparams(("arbitrary", "arbitrary", "arbitrary")),
        name="diff_attn",
    )(tab, p, p, p, lq1.reshape(1, -1), lk1.reshape(1, -1), lq2.reshape(1, -1), lk2.reshape(1, -1),
      g_subln.reshape(-1, 1))


DIL_TQ = 128
DIL_PAD = DIL_HALF_SPAN
DIL_WIN = DIL_TQ + 2 * DIL_PAD


def _dil_kernel(tab_ref, q_ref, k_ref, v_ref, o_ref, lse_ref, kpad_ref, vpad_ref, mask_ref, *, L, LQ, dilation):
    TQ, W, hs = DIL_TQ, DIL_WIN, DIL_HALF_SPAN
    n_tiles = L // TQ
    qblock = pl.program_id(2)

    @pl.when((pl.program_id(0) == 0) & (pl.program_id(1) == 0) & (qblock == 0))
    def _():
        row = lax.broadcasted_iota(jnp.int32, (TQ, W), 0)
        col = lax.broadcasted_iota(jnp.int32, (TQ, W), 1)
        rel = col - DIL_PAD - row
        band = jnp.abs(rel) <= hs
        for variant, qs in enumerate((0, TQ, L - TQ)):
            kpos = qs - DIL_PAD + col
            ok = band & (kpos >= 0) & (kpos < L)
            for h in range(DIL_HEADS):
                mask_ref[variant, h] = jnp.where(ok, _bias_of_rel(dilation * rel, tab_ref, h, DIL_HEADS), NEG_BIG)

    @pl.when(qblock == 0)
    def _():
        pad = jnp.zeros((DIL_PAD, DIL_COLS), BF16)
        for ref, src in ((kpad_ref, k_ref), (vpad_ref, v_ref)):
            ref[0:DIL_PAD, :] = pad
            ref[DIL_PAD + L:2 * DIL_PAD + L, :] = pad
            ref[DIL_PAD:DIL_PAD + L, :] = src[0]

    lane = lax.broadcasted_iota(jnp.int32, (TQ, LANES), 1)
    low = lane < DIL_HEAD_DIM

    def tile(t, _):
        qs = pl.multiple_of(t * TQ, TQ)
        tg = qblock * (LQ // TQ) + t
        ks = pl.multiple_of(tg * TQ, TQ)
        variant = jnp.where(tg == 0, 0, jnp.where(tg == n_tiles - 1, 2, 1))
        lse_tile = jnp.zeros((TQ, LANES), F32)
        scores = []
        for h in range(DIL_HEADS):
            cols = slice(h // 2 * LANES, (h // 2 + 1) * LANES)
            q = q_ref[0, pl.ds(qs, TQ), cols].astype(F32) * (DIL_HEAD_DIM ** -0.5)
            qm = jnp.where(low if h % 2 == 0 else ~low, q, 0.0).astype(BF16)
            scores.append(_dot_nt(qm, kpad_ref[pl.ds(ks, W), cols]) + mask_ref[variant, h])
        probs, dens = [], []
        for h in range(DIL_HEADS):
            m = jnp.max(scores[h], axis=-1, keepdims=True)
            pr = jnp.exp(scores[h] - m)
            den = jnp.sum(pr, axis=-1, keepdims=True)
            probs.append(pr.astype(BF16))
            dens.append(den)
            lse_tile = jnp.where(lane == h, m + jnp.log(den), lse_tile)
        for hp in range(DIL_HEADS // 2):
            cols = slice(hp * LANES, (hp + 1) * LANES)
            vw = vpad_ref[pl.ds(ks, W), cols]
            outs = [jnp.dot(probs[2 * hp + e], vw, preferred_element_type=F32) / dens[2 * hp + e] for e in range(2)]
            o_ref[0, pl.ds(qs, TQ), cols] = jnp.where(low, outs[0], outs[1])
        lse_ref[0, pl.ds(qs, TQ), :] = lse_tile
        return 0

    lax.fori_loop(0, LQ // TQ, tile, 0)


def dilated_group(pg, tab, qkv_col, dilation, lq=2048):
    B, r, L, _ = pg.shape
    assert r == dilation and L >= 2 * DIL_TQ and L % DIL_TQ == 0
    LQ = min(lq, L)

    def col(which, rows, tiled):
        cb = qkv_col // DIL_COLS + which
        return pl.BlockSpec((1, None, rows, DIL_COLS), lambda b, rho, t: (b, rho, t if tiled else 0, cb))

    return pl.pallas_call(
        functools.partial(_dil_kernel, L=L, LQ=LQ, dilation=r),
        grid=(B, r, L // LQ),
        in_specs=[pl.BlockSpec(memory_space=pltpu.SMEM),
                  col(0, LQ, True), col(1, L, False), col(2, L, False)],
        out_specs=[pl.BlockSpec((1, None, LQ, DIL_COLS), lambda b, rho, t: (b, rho, t, 0)),
                   pl.BlockSpec((1, None, LQ, LANES), lambda b, rho, t: (b, rho, t, 0))],
        out_shape=[jax.ShapeDtypeStruct((B, r, L, DIL_COLS), F32),
                   jax.ShapeDtypeStruct((B, r, L, LANES), F32)],
        scratch_shapes=[pltpu.VMEM((L + 2 * DIL_PAD, DIL_COLS), BF16),
                        pltpu.VMEM((L + 2 * DIL_PAD, DIL_COLS), BF16),
                        pltpu.VMEM((3, DIL_HEADS, DIL_TQ, DIL_WIN), F32)],
        compiler_params=_cparams(("arbitrary", "arbitrary", "arbitrary")),
        name=f"dilated_r{r}",
    )(tab, pg, pg, pg)


def _split_dot(w, e_ref):
    hi = w.astype(BF16)
    lo = (w - hi.astype(F32)).astype(BF16)
    e = e_ref[...]
    return jnp.dot(hi, e, preferred_element_type=F32) + jnp.dot(lo, e, preferred_element_type=F32)


def _token_order(src_ref, tmp_ref, r):
    if r == 1:
        return src_ref[0, 0]
    n, width = src_ref.shape[2], src_ref.shape[3]
    for c in range(width // LANES):
        for rho in range(r):
            tmp_ref[c, pl.ds(rho, n, stride=r), :] = src_ref[0, rho, :, c * LANES:(c + 1) * LANES]
    return jnp.concatenate([tmp_ref[c] for c in range(width // LANES)], axis=1)


def _merge_kernel(x_ref, mod_ref, oa_ref, o0_ref, o1_ref, o2_ref, l0_ref, l1_ref, l2_ref,
                  ga0_ref, ga1_ref, gb0_ref, gb1_ref, e_ref, wpa_ref, wpb_ref, wout_ref, gffn_ref,
                  x1_ref, h2_ref, to1_ref, to2_ref, tl1_ref, tl2_ref):
    dil = [d for _, d in DIL_GROUPS]
    a = jnp.dot(oa_ref[0], wpa_ref[...], preferred_element_type=F32)
    l0 = _token_order(l0_ref, None, dil[0])
    l1 = _token_order(l1_ref, tl1_ref, dil[1])
    l2 = _token_order(l2_ref, tl2_ref, dil[2])
    mx = jnp.maximum(jnp.maximum(l0, l1), l2)
    e0, e1, e2 = jnp.exp(l0 - mx), jnp.exp(l1 - mx), jnp.exp(l2 - mx)
    den = e0 + e1 + e2
    ob = (_token_order(o0_ref, None, dil[0]) * _split_dot(e0 / den, e_ref)
          + _token_order(o1_ref, to1_ref, dil[1]) * _split_dot(e1 / den, e_ref)
          + _token_order(o2_ref, to2_ref, dil[2]) * _split_dot(e2 / den, e_ref))
    b = jnp.dot(ob.astype(BF16), wpb_ref[...], preferred_element_type=F32)
    ga = jnp.concatenate([ga0_ref[0], ga1_ref[0]], axis=-1).astype(F32)
    gb = jnp.concatenate([gb0_ref[0], gb1_ref[0]], axis=-1).astype(F32)
    merged = a / (1.0 + jnp.exp(-ga)) + b / (1.0 + jnp.exp(-gb))
    mix = jnp.dot(merged.astype(BF16), wout_ref[...], preferred_element_type=F32)
    x1 = x_ref[0] + mod_ref[0, 2:3, :] * mix
    x1_ref[0] = x1
    h2_ref[0] = _modulated_norm(x1, gffn_ref[...], mod_ref[0, 4:5, :], mod_ref[0, 3:4, :]).astype(BF16)


def merge_proj(x, mod, p_nat, oa, outs, lses, wpa, wpb, wout, g_ffn, tm=512):
    B, S, _ = x.shape
    tm = min(tm, S)
    expand = np.zeros((LANES, DIL_COLS), np.float32)
    for h in range(DIL_HEADS):
        expand[h, h * DIL_HEAD_DIM:(h + 1) * DIL_HEAD_DIM] = 1.0
    expand = jnp.asarray(expand, BF16)

    def tok(w):
        return pl.BlockSpec((1, tm, w), lambda b, i: (b, i, 0))

    def res(w, r):
        return pl.BlockSpec((1, r, tm // r, w), lambda b, i: (b, 0, i, 0))

    def pcol(off):
        return pl.BlockSpec((1, tm, DIL_COLS), lambda b, i: (b, i, off // DIL_COLS))

    def whole(a):
        return pl.BlockSpec(a.shape, lambda b, i: (0,) * a.ndim)

    g = g_ffn.reshape(1, -1)
    dil = [d for _, d in DIL_GROUPS]
    return pl.pallas_call(
        _merge_kernel,
        grid=(B, S // tm),
        in_specs=[tok(D_MODEL), pl.BlockSpec((1, 6, D_MODEL), lambda b, i: (b, 0, 0)), tok(DA_V),
                  res(DIL_COLS, dil[0]), res(DIL_COLS, dil[1]), res(DIL_COLS, dil[2]),
                  res(LANES, dil[0]), res(LANES, dil[1]), res(LANES, dil[2]),
                  pcol(NAT_GA), pcol(NAT_GA + DIL_COLS), pcol(NAT_GB), pcol(NAT_GB + DIL_COLS),
                  whole(expand), whole(wpa), whole(wpb), whole(wout), whole(g)],
        out_specs=[tok(D_MODEL), tok(D_MODEL)],
        out_shape=[jax.ShapeDtypeStruct((B, S, D_MODEL), F32), jax.ShapeDtypeStruct((B, S, D_MODEL), BF16)],
        scratch_shapes=[pltpu.VMEM((DIL_COLS // LANES, tm, LANES), F32),
                        pltpu.VMEM((DIL_COLS // LANES, tm, LANES), F32),
                        pltpu.VMEM((1, tm, LANES), F32), pltpu.VMEM((1, tm, LANES), F32)],
        compiler_params=_cparams(("arbitrary", "arbitrary")),
        name="merge_proj",
    )(x, mod, oa, outs[0], outs[1], outs[2], lses[0], lses[1], lses[2], p_nat, p_nat, p_nat, p_nat,
      expand, wpa, wpb, wout, g)


def _top_values(s, count, ranks=False):
    row = lax.broadcasted_iota(jnp.int32, (count, s.shape[1]), 0)
    rows = []
    top = jnp.zeros((count, s.shape[1]), F32)
    rank = jnp.full(s.shape, float(count), F32)
    for i in range(count):
        m = jnp.max(s, axis=0, keepdims=True)
        rows.append(m)
        top = jnp.where(row == i, m, top)
        hit = s >= m
        if ranks:
            rank = jnp.where(hit, float(i), rank)
        s = jnp.where(hit, -jnp.inf, s)
    return (rows, top, rank) if ranks else (rows, top)


def _paired_bf16(x):
    bits = pltpu.bitcast(x.astype(BF16).astype(F32), jnp.uint32)
    return bits | (bits >> 16)


def _route_kernel(h_ref, wq_ref, k1_ref, k2_ref, r2_ref, e2_ref, cnt_ref, e1_ref):
    qp = jnp.dot(h_ref[...], wq_ref[...], preferred_element_type=F32)
    tm = qp.shape[0]
    K, half = PEER_TOPK, PEER_TOPK // 2
    first_row = lax.broadcasted_iota(jnp.int32, (K, tm), 0) == 0
    for h in range(PEER_HEADS):
        q1 = qp[:, h * PEER_QUERY_DIM:h * PEER_QUERY_DIM + PEER_HALF].astype(BF16)
        q2 = qp[:, h * PEER_QUERY_DIM + PEER_HALF:(h + 1) * PEER_QUERY_DIM].astype(BF16)
        s1 = _dot_nt(k1_ref[h], q1)
        s2 = _dot_nt(k2_ref[h], q2)
        r1, v1 = _top_values(s1, K)
        r2, v2, rank2 = _top_values(s2, K, ranks=True)
        pieces = [v1 + r2[0]] + [v1[0:half] + r2[j] for j in range(1, half)] + [v2[half:K] + r1[0]]
        cand = jnp.concatenate(pieces, axis=0)
        tau = _top_values(cand, K)[0][K - 1]
        top = r1[0] + r2[0]
        z = jnp.sum(jnp.where(cand >= tau, jnp.exp(cand - top), 0.0), axis=0, keepdims=True)
        reach = [jnp.where(p >= tau, 1.0, 0.0) for p in pieces]
        low = reach[1]
        for p in reach[2:half]:
            low = low + p
        tail = jnp.sum(reach[half], axis=0, keepdims=True)
        cnt_rank = reach[0] + jnp.concatenate([low, jnp.zeros_like(low)], axis=0) + jnp.where(first_row, tail, 0.0)
        cnt = jnp.zeros(s1.shape, F32)
        for i in range(K):
            cnt = jnp.where(s1 == r1[i], cnt_rank[i:i + 1], cnt)
        r2_ref[h] = rank2.astype(BF16)
        e2_ref[h] = (jnp.exp(s2 - r2[0]) / z).astype(BF16)
        cnt_ref[h] = _paired_bf16(cnt)
        e1_ref[h] = _paired_bf16(0.5 * jnp.exp(s1 - r1[0]))


def peer_route(h2, wq, k1, k2, tm=256):
    T = h2.shape[0]
    tm = min(tm, T)
    half = jax.ShapeDtypeStruct((PEER_HEADS, N_KEYS, T), BF16)
    word = jax.ShapeDtypeStruct((PEER_HEADS, N_KEYS, T), jnp.uint32)
    bspec = pl.BlockSpec((PEER_HEADS, N_KEYS, tm), lambda i: (0, 0, i))
    kspec = pl.BlockSpec((PEER_HEADS, N_KEYS, PEER_HALF), lambda i: (0, 0, 0))
    return pl.pallas_call(
        _route_kernel,
        grid=(T // tm,),
        in_specs=[pl.BlockSpec((tm, D_MODEL), lambda i: (i, 0)),
                  pl.BlockSpec(wq.shape, lambda i: (0, 0)), kspec, kspec],
        out_specs=[bspec, bspec, bspec, bspec],
        out_shape=[half, half, word, word],
        compiler_params=_cparams(("arbitrary",)),
        name="peer_route",
    )(h2, wq, k1, k2)


PEER_TE = 2048
PEER_SLAB_GROUP = 8
PACK = 16


def _dense_kernel(h_ref, u_ref, vt_ref, r2_ref, e2_ref, cnt_ref, e1_ref, x1_ref, mod_ref, gfin_ref,
                  y_ref, acc_ref, a_ref):
    j = pl.program_id(1)
    tm = h_ref.shape[0]

    @pl.when(j == 0)
    def _():
        acc_ref[...] = jnp.zeros_like(acc_ref)

    def rows_bf16(ref, h, first):
        block = ref[h, pl.ds(first, PEER_SLAB_GROUP), :]
        packed = pltpu.bitcast(block, BF16)
        return [jnp.broadcast_to(packed[2 * al:2 * al + 1, :], (PACK, tm)) for al in range(PEER_SLAB_GROUP)]

    zero = jnp.zeros((PACK, tm), BF16)
    group_rows = PEER_SLAB_GROUP * N_KEYS
    n_groups = PEER_TE // group_rows
    zts = [_dot_nt(u_ref[g * group_rows:(g + 1) * group_rows, :], h_ref[...]) for g in range(n_groups)]
    for sg in range(n_groups):
        experts = slice(sg * group_rows, (sg + 1) * group_rows)
        zt = zts[sg]
        first = pl.multiple_of(j * (PEER_TE // N_KEYS) + sg * PEER_SLAB_GROUP, PEER_SLAB_GROUP)
        cnt_rows = [rows_bf16(cnt_ref, h, first) for h in range(PEER_HEADS)]
        e1_rows = [rows_bf16(e1_ref, h, first) for h in range(PEER_HEADS)]
        for c in range(N_KEYS // PACK):
            keys = slice(c * PACK, (c + 1) * PACK)
            w = [None] * PEER_SLAB_GROUP
            for h in range(PEER_HEADS):
                r2 = r2_ref[h, keys, :]
                e2 = e2_ref[h, keys, :]
                for al in range(PEER_SLAB_GROUP):
                    term = jnp.where(r2 < cnt_rows[h][al], e2, zero) * e1_rows[h][al]
                    w[al] = term if w[al] is None else w[al] + term
            for al in range(PEER_SLAB_GROUP):
                base = al * N_KEYS + c * PACK
                z = zt[base:base + PACK, :]
                a_ref[sg * group_rows + base:sg * group_rows + base + PACK, :] = (
                    (z * (1.0 + lax.erf(z * (2.0 ** -0.5)))).astype(BF16) * w[al])
        acc_ref[...] += jnp.dot(vt_ref[:, experts], a_ref[experts, :], preferred_element_type=F32)

    @pl.when(j == pl.num_programs(1) - 1)
    def _():
        x2 = x1_ref[...] + mod_ref[0, 5:6, :] * acc_ref[...].T
        ms = jnp.mean(x2 * x2, axis=-1, keepdims=True)
        y_ref[...] = x2 * lax.rsqrt(ms + EPS) * gfin_ref[...]


def peer_dense(h2, u_bf16, vt_bf16, routes, x1, mod, g_final, S, tm=512):
    T = h2.shape[0]
    tm = min(tm, S)
    r2, e2, cnt, e1 = routes
    rspec = pl.BlockSpec((PEER_HEADS, N_KEYS, tm), lambda i, j: (0, 0, i))
    tiles_per_row = S // tm
    return pl.pallas_call(
        _dense_kernel,
        grid=(T // tm, N_EXPERTS // PEER_TE),
        in_specs=[pl.BlockSpec((tm, D_MODEL), lambda i, j: (i, 0)),
                  pl.BlockSpec((PEER_TE, D_MODEL), lambda i, j: (j, 0)),
                  pl.BlockSpec((D_MODEL, PEER_TE), lambda i, j: (0, j)),
                  rspec, rspec, rspec, rspec,
                  pl.BlockSpec((tm, D_MODEL), lambda i, j: (i, 0)),
                  pl.BlockSpec((1, 6, D_MODEL), lambda i, j: (i // tiles_per_row, 0, 0)),
                  pl.BlockSpec((1, D_MODEL), lambda i, j: (0, 0))],
        out_specs=pl.BlockSpec((tm, D_MODEL), lambda i, j: (i, 0)),
        out_shape=jax.ShapeDtypeStruct((T, D_MODEL), F32),
        scratch_shapes=[pltpu.VMEM((D_MODEL, tm), F32), pltpu.VMEM((PEER_TE, tm), BF16)],
        compiler_params=_cparams(("arbitrary", "arbitrary")),
        name="peer_dense",
    )(h2, u_bf16, vt_bf16, r2, e2, cnt, e1, x1, mod, g_final.reshape(1, -1))


def _prepare_weights(w_in, w_proj_a, w_proj_b, w_out, rel_bias, w_query, sub_keys_1, sub_keys_2,
                     expert_u, expert_v):
    tabs = [rel_bias[:, :DA_HEADS].reshape(-1)]
    for gi in range(len(DIL_GROUPS)):
        lo = DA_HEADS + gi * DIL_HEADS
        tabs.append(rel_bias[:, lo:lo + DIL_HEADS].reshape(-1))
    w = w_in[0].astype(BF16)

    def group_cols(gi):
        return [w[:, off + gi * DIL_COLS:off + (gi + 1) * DIL_COLS] for off in (OFF_QB, OFF_KB, OFF_VB)]

    w_nat = jnp.concatenate([w[:, :OFF_QB]] + group_cols(0) + [w[:, OFF_GA:]], axis=1)
    w_groups = [jnp.concatenate(group_cols(gi), axis=1) for gi in range(1, len(DIL_GROUPS))]
    return dict(
        w_nat=w_nat, w_groups=w_groups, wpa=w_proj_a[0].astype(BF16), wpb=w_proj_b[0].astype(BF16),
        wout=w_out[0].astype(BF16), tabs=tabs, wq=w_query[0].astype(BF16),
        k1=sub_keys_1[0].astype(BF16), k2=sub_keys_2[0].astype(BF16),
        u=expert_u[0].astype(BF16), vt=expert_v[0].astype(BF16).T)


def _encoder(x, c, w_ada, b_ada, g_norm_mix, g_norm_ffn, lq1, lk1, lq2, lk2, g_subln, g_final, wts):
    B, S, _ = x.shape
    mod = ada_mod(c, w_ada[0], b_ada[0]).reshape(B, 6, D_MODEL)
    p_nat = in_proj(x, mod, g_norm_mix[0], wts["w_nat"], tn=NAT_COLS // 4)
    oa = diff_attn(p_nat, wts["tabs"][0], lq1[0], lk1[0], lq2[0], lk2[0], g_subln[0])
    outs, lses = [], []
    for gi, (_, dilation) in enumerate(DIL_GROUPS):
        if gi == 0:
            pg, qkv_col = p_nat.reshape(B, 1, S, NAT_COLS), NAT_Q0
        else:
            pg, qkv_col = in_proj(x, mod, g_norm_mix[0], wts["w_groups"][gi - 1], r=dilation), 0
        o_g, lse_g = dilated_group(pg, wts["tabs"][1 + gi], qkv_col, dilation)
        outs.append(o_g)
        lses.append(lse_g)
    x1, h2 = merge_proj(x, mod, p_nat, oa, outs, lses, wts["wpa"], wts["wpb"], wts["wout"], g_norm_ffn[0])
    h2 = h2.reshape(B * S, D_MODEL)
    routes = peer_route(h2, wts["wq"], wts["k1"], wts["k2"])
    y = peer_dense(h2, wts["u"], wts["vt"], routes, x1.reshape(B * S, D_MODEL), mod, g_final, S)
    return y.reshape(B, S, D_MODEL)


def kernel(x_prompt, x_sample, c_prompt, c_sample, w_ada, b_ada, g_norm_mix, g_norm_ffn, w_in, w_proj_a, w_proj_b, w_out, lambda_q1, lambda_k1, lambda_q2, lambda_k2, g_subln, rel_bias, w_query, sub_keys_1, sub_keys_2, expert_u, expert_v, g_final):
    wts = _prepare_weights(w_in, w_proj_a, w_proj_b, w_out, rel_bias, w_query, sub_keys_1, sub_keys_2,
                           expert_u, expert_v)
    args = (w_ada, b_ada, g_norm_mix, g_norm_ffn, lambda_q1, lambda_k1, lambda_q2, lambda_k2, g_subln, g_final, wts)
    return _encoder(x_prompt, c_prompt, *args), _encoder(x_sample, c_sample, *args)
```

```python
import functools
import math

import jax
import jax.numpy as jnp
import numpy as np
from jax import lax
from jax.experimental import pallas as pl
from jax.experimental.pallas import tpu as pltpu

F32 = jnp.float32
BF16 = jnp.bfloat16

D_MODEL = 1024
DA_HEADS = 8
DA_HEAD_DIM = 64
DA_QK = DA_HEADS * 2 * DA_HEAD_DIM
DA_V = DA_HEADS * 2 * DA_HEAD_DIM
DIL_GROUPS = ((128, 1), (512, 4), (2048, 16))
DIL_HEADS = 8
DIL_HEAD_DIM = 64
DIL_HALF_SPAN = 64
DIL_COLS = DIL_HEADS * DIL_HEAD_DIM
DIL_QKV = len(DIL_GROUPS) * DIL_COLS
IN_COLS = 2 * DA_QK + DA_V + 3 * DIL_QKV + 2 * D_MODEL
NUM_BUCKETS = 32
N_REL_HEADS = DA_HEADS + len(DIL_GROUPS) * DIL_HEADS
PEER_HEADS = 8
N_KEYS = 128
N_EXPERTS = N_KEYS * N_KEYS
PEER_QUERY_DIM = 256
PEER_HALF = PEER_QUERY_DIM // 2
PEER_TOPK = 16
EPS = 1e-6
NEG_BIG = -1e30
LAM_INIT = 0.8 - 0.6 * math.exp(-0.3 * 0)

LANES = 128
VMEM_LIMIT = 56 * 1024 * 1024

OFF_QA, OFF_KA, OFF_VA = 0, DA_QK, 2 * DA_QK
OFF_QB = 2 * DA_QK + DA_V
OFF_KB = OFF_QB + DIL_QKV
OFF_VB = OFF_KB + DIL_QKV
OFF_GA = OFF_VB + DIL_QKV
OFF_GB = OFF_GA + D_MODEL
NAT_Q0 = OFF_QB
NAT_GA = NAT_Q0 + 3 * DIL_COLS
NAT_GB = NAT_GA + D_MODEL
NAT_COLS = NAT_GB + D_MODEL

BUCKET_THRESHOLDS = (1, 2, 3, 4, 5, 6, 7, 8, 15, 27, 50, 91, 166, 305, 559)
BUCKET_CONST_FROM = BUCKET_THRESHOLDS[-1]


def _cparams(sem):
    return pltpu.CompilerParams(dimension_semantics=sem, vmem_limit_bytes=VMEM_LIMIT)


def _dot_nt(a, b):
    return lax.dot_general(a, b, (((1,), (1,)), ((), ())), preferred_element_type=F32)


def _bias_of_rel(rel, tab_ref, head, n_heads):
    n = jnp.abs(rel)
    neg = jnp.full(rel.shape, tab_ref[0 * n_heads + head], F32)
    pos = jnp.full(rel.shape, tab_ref[16 * n_heads + head], F32)
    for k, thr in enumerate(BUCKET_THRESHOLDS, start=1):
        ge = n >= thr
        neg = jnp.where(ge, tab_ref[k * n_heads + head], neg)
        pos = jnp.where(ge, tab_ref[(16 + k) * n_heads + head], pos)
    return jnp.where(rel > 0, pos, neg)


def _ada_kernel(c_ref, w_ref, b_ref, o_ref):
    c = c_ref[...]
    sc = c / (1.0 + jnp.exp(-c))
    o_ref[...] = jnp.dot(sc, w_ref[...], preferred_element_type=F32,
                         precision=lax.Precision.HIGHEST) + b_ref[...]


def ada_mod(c, w_ada, b_ada):
    B = c.shape[0]
    n = w_ada.shape[1] // D_MODEL
    return pl.pallas_call(
        _ada_kernel,
        grid=(n,),
        in_specs=[pl.BlockSpec((B, D_MODEL), lambda j: (0, 0)),
                  pl.BlockSpec((D_MODEL, D_MODEL), lambda j: (0, j)),
                  pl.BlockSpec((1, D_MODEL), lambda j: (0, j))],
        out_specs=pl.BlockSpec((B, D_MODEL), lambda j: (0, j)),
        out_shape=jax.ShapeDtypeStruct((B, n * D_MODEL), F32),
        compiler_params=_cparams(("arbitrary",)),
        name="ada_mod",
    )(c, w_ada, b_ada.reshape(1, -1))


def _modulated_norm(x, g, scale, shift):
    ms = jnp.mean(x * x, axis=-1, keepdims=True)
    return (x * lax.rsqrt(ms + EPS) * g) * (1.0 + scale) + shift


def _in_proj_kernel(x_ref, mod_ref, g_ref, w_ref, o_ref, h_ref, *hs_ref, r):
    tm = h_ref.shape[0]
    n = tm // r

    @pl.when(pl.program_id(2) == 0)
    def _():
        h = _modulated_norm(x_ref[0], g_ref[...], mod_ref[0, 1:2, :], mod_ref[0, 0:1, :])
        if r == 1:
            h_ref[...] = h.astype(BF16)
        else:
            for c in range(D_MODEL // LANES):
                hs_ref[0][c] = h[:, c * LANES:(c + 1) * LANES]
            for c in range(D_MODEL // LANES):
                h_ref[:, c * LANES:(c + 1) * LANES] = jnp.concatenate(
                    [hs_ref[0][c, pl.ds(rho, n, stride=r), :] for rho in range(r)], axis=0).astype(BF16)

    res = jnp.dot(h_ref[...], w_ref[...], preferred_element_type=F32).astype(BF16)
    if r == 1:
        o_ref[0] = res
    else:
        for rho in range(r):
            o_ref[0, rho] = res[rho * n:(rho + 1) * n, :]


def in_proj(x, mod, g_norm, w_bf16, r=1, tm=2048, tn=None):
    B, S, _ = x.shape
    N = w_bf16.shape[1]
    tn = N if tn is None else tn
    tm = min(tm, S)
    if r == 1:
        out_spec = pl.BlockSpec((1, tm, tn), lambda b, i, j: (b, i, j))
        out_shape = jax.ShapeDtypeStruct((B, S, N), BF16)
        scratch = [pltpu.VMEM((tm, D_MODEL), BF16)]
    else:
        out_spec = pl.BlockSpec((1, r, tm // r, tn), lambda b, i, j: (b, 0, i, j))
        out_shape = jax.ShapeDtypeStruct((B, r, S // r, N), BF16)
        scratch = [pltpu.VMEM((tm, D_MODEL), BF16), pltpu.VMEM((D_MODEL // LANES, tm, LANES), F32)]
    return pl.pallas_call(
        functools.partial(_in_proj_kernel, r=r),
        grid=(B, S // tm, N // tn),
        in_specs=[pl.BlockSpec((1, tm, D_MODEL), lambda b, i, j: (b, i, 0)),
                  pl.BlockSpec((1, 6, D_MODEL), lambda b, i, j: (b, 0, 0)),
                  pl.BlockSpec((1, D_MODEL), lambda b, i, j: (0, 0)),
                  pl.BlockSpec((D_MODEL, tn), lambda b, i, j: (0, j))],
        out_specs=out_spec,
        out_shape=out_shape,
        scratch_shapes=scratch,
        compiler_params=_cparams(("arbitrary", "arbitrary", "arbitrary")),
        name=f"in_proj_r{r}",
    )(x, mod, g_norm.reshape(1, -1), w_bf16)


DA_TQ = 512
DA_TK = 512
DA_D_LO = -(-(BUCKET_CONST_FROM - 1 + DA_TK) // DA_TQ)
DA_D_HI = -(-(BUCKET_CONST_FROM - 1 + DA_TQ) // DA_TQ)
DA_NBIAS = DA_D_LO + DA_D_HI + 1
DA_VALUE_PARTS = 1
DA_VROWS = 144
LOG2E = 1.4426950408889634


def _da_kernel(tab_ref, q_ref, k_ref, v_ref, lq1_ref, lk1_ref, lq2_ref, lk2_ref, gsub_ref, o_ref,
               bias_ref, vt_ref, s_ref, p_ref, *, n_chunks):
    TQ, TK = DA_TQ, DA_TK
    head = pl.program_id(0)
    qi = pl.program_id(2)

    @pl.when((pl.program_id(1) == 0) & (qi == 0))
    def _():
        row = lax.broadcasted_iota(jnp.int32, (TK, TQ), 0)
        col = lax.broadcasted_iota(jnp.int32, (TK, TQ), 1)
        for d in range(DA_NBIAS):
            bias_ref[d] = _bias_of_rel((d - DA_D_LO) * TQ + row - col, tab_ref, head, DA_HEADS) * LOG2E

    @pl.when(qi == 0)
    def _():
        extra = lax.broadcasted_iota(jnp.int32, (DA_VROWS - LANES, TK), 0)
        ones_row = jnp.where(extra == 0, 1.0, 0.0).astype(BF16)
        for c in range(n_chunks):
            vt_ref[0:LANES, c * TK:(c + 1) * TK] = v_ref[0, c * TK:(c + 1) * TK, :].astype(F32).T.astype(BF16)
            vt_ref[LANES:DA_VROWS, c * TK:(c + 1) * TK] = ones_row

    q = q_ref[0].astype(F32) * (DA_HEAD_DIM ** -0.5 * LOG2E)
    lane = lax.broadcasted_iota(jnp.int32, q.shape, 1)
    q1 = jnp.where(lane < DA_HEAD_DIM, q, 0.0).astype(BF16)
    q2 = jnp.where(lane >= DA_HEAD_DIM, q, 0.0).astype(BF16)

    def score_chunk(kc, carry):
        m1, m2 = carry
        k = k_ref[0, pl.ds(pl.multiple_of(kc * TK, TK), TK), :]
        d = kc * (TK // TQ) - qi
        bias = bias_ref[jnp.clip(d, -DA_D_LO, DA_D_HI) + DA_D_LO]
        s1 = _dot_nt(k, q1) + bias
        s2 = _dot_nt(k, q2) + bias
        s_ref[0, kc] = s1
        s_ref[1, kc] = s2
        return (jnp.maximum(m1, jnp.max(s1, axis=0, keepdims=True)),
                jnp.maximum(m2, jnp.max(s2, axis=0, keepdims=True)))

    neg = jnp.full((1, TQ), -jnp.inf, F32)
    m1, m2 = lax.fori_loop(0, n_chunks, score_chunk, (neg, neg), unroll=4)

    per_part = n_chunks // DA_VALUE_PARTS
    a1 = a2 = None
    for part in range(DA_VALUE_PARTS):
        for c in range(part * per_part, (part + 1) * per_part):
            p_ref[0, c * TK:(c + 1) * TK, :] = jnp.exp2((s_ref[0, c] - m1).astype(BF16))
            p_ref[1, c * TK:(c + 1) * TK, :] = jnp.exp2((s_ref[1, c] - m2).astype(BF16))
        keys = slice(part * per_part * TK, (part + 1) * per_part * TK)
        d1 = jnp.dot(vt_ref[:, keys], p_ref[0, keys, :], preferred_element_type=F32)
        d2 = jnp.dot(vt_ref[:, keys], p_ref[1, keys, :], preferred_element_type=F32)
        a1 = d1 if a1 is None else a1 + d1
        a2 = d2 if a2 is None else a2 + d2

    lam = (jnp.exp(jnp.sum(lq1_ref[...] * lk1_ref[...], axis=-1, keepdims=True))
           - jnp.exp(jnp.sum(lq2_ref[...] * lk2_ref[...], axis=-1, keepdims=True)) + LAM_INIT)
    ot = a1[0:LANES] / a1[LANES:LANES + 1] - lam * (a2[0:LANES] / a2[LANES:LANES + 1])
    ms = jnp.mean(ot * ot, axis=0, keepdims=True)
    y = ot * lax.rsqrt(ms + EPS) * gsub_ref[...] * (1.0 - LAM_INIT)
    o_ref[0] = y.T.astype(BF16)


def diff_attn(p, tab, lq1, lk1, lq2, lk2, g_subln):
    B, S, _ = p.shape
    TQ, TK = DA_TQ, DA_TK
    n_chunks = S // TK
    small = pl.BlockSpec((1, DA_HEAD_DIM), lambda h, b, i: (0, 0))
    return pl.pallas_call(
        functools.partial(_da_kernel, n_chunks=n_chunks),
        grid=(DA_HEADS, B, S // TQ),
        in_specs=[pl.BlockSpec(memory_space=pltpu.SMEM),
                  pl.BlockSpec((1, TQ, LANES), lambda h, b, i: (b, i, OFF_QA // LANES + h)),
                  pl.BlockSpec((1, S, LANES), lambda h, b, i: (b, 0, OFF_KA // LANES + h)),
                  pl.BlockSpec((1, S, LANES), lambda h, b, i: (b, 0, OFF_VA // LANES + h)),
                  small, small, small, small,
                  pl.BlockSpec((2 * DA_HEAD_DIM, 1), lambda h, b, i: (0, 0))],
        out_specs=pl.BlockSpec((1, TQ, LANES), lambda h, b, i: (b, i, h)),
        out_shape=jax.ShapeDtypeStruct((B, S, DA_V), BF16),
        scratch_shapes=[pltpu.VMEM((DA_NBIAS, TK, TQ), F32),
                        pltpu.VMEM((DA_VROWS, S), BF16),
                        pltpu.VMEM((2, n_chunks, TK, TQ), F32),
                        pltpu.VMEM((2, S, TQ), BF16)],
        compiler_params=_cparams(("arbitrary", "arbitrary", "arbitrary")),
        name="diff_attn",
    )(tab, p, p, p, lq1.reshape(1, -1), lk1.reshape(1, -1), lq2.reshape(1, -1), lk2.reshape(1, -1),
      g_subln.reshape(-1, 1))


DIL_TQ = 128
DIL_PAD = DIL_HALF_SPAN
DIL_WIN = DIL_TQ + 2 * DIL_PAD


def _dil_kernel(tab_ref, q_ref, k_ref, v_ref, o_ref, lse_ref, kpad_ref, vpad_ref, mask_ref, *, L, LQ, dilation):
    TQ, W, hs = DIL_TQ, DIL_WIN, DIL_HALF_SPAN
    n_tiles = L // TQ
    qblock = pl.program_id(2)

    @pl.when((pl.program_id(0) == 0) & (pl.program_id(1) == 0) & (qblock == 0))
    def _():
        row = lax.broadcasted_iota(jnp.int32, (TQ, W), 0)
        col = lax.broadcasted_iota(jnp.int32, (TQ, W), 1)
        rel = col - DIL_PAD - row
        band = jnp.abs(rel) <= hs
        for variant, qs in enumerate((0, TQ, L - TQ)):
            kpos = qs - DIL_PAD + col
            ok = band & (kpos >= 0) & (kpos < L)
            for h in range(DIL_HEADS):
                mask_ref[variant, h] = jnp.where(ok, _bias_of_rel(dilation * rel, tab_ref, h, DIL_HEADS), NEG_BIG)

    @pl.when(qblock == 0)
    def _():
        pad = jnp.zeros((DIL_PAD, DIL_COLS), BF16)
        for ref, src in ((kpad_ref, k_ref), (vpad_ref, v_ref)):
            ref[0:DIL_PAD, :] = pad
            ref[DIL_PAD + L:2 * DIL_PAD + L, :] = pad
            ref[DIL_PAD:DIL_PAD + L, :] = src[0]

    lane = lax.broadcasted_iota(jnp.int32, (TQ, LANES), 1)
    low = lane < DIL_HEAD_DIM

    def tile(t, _):
        qs = pl.multiple_of(t * TQ, TQ)
        tg = qblock * (LQ // TQ) + t
        ks = pl.multiple_of(tg * TQ, TQ)
        variant = jnp.where(tg == 0, 0, jnp.where(tg == n_tiles - 1, 2, 1))
        lse_tile = jnp.zeros((TQ, LANES), F32)
        scores = []
        for h in range(DIL_HEADS):
            cols = slice(h // 2 * LANES, (h // 2 + 1) * LANES)
            q = q_ref[0, pl.ds(qs, TQ), cols].astype(F32) * (DIL_HEAD_DIM ** -0.5)
            qm = jnp.where(low if h % 2 == 0 else ~low, q, 0.0).astype(BF16)
            scores.append(_dot_nt(qm, kpad_ref[pl.ds(ks, W), cols]) + mask_ref[variant, h])
        probs, dens = [], []
        for h in range(DIL_HEADS):
            m = jnp.max(scores[h], axis=-1, keepdims=True)
            pr = jnp.exp(scores[h] - m)
            den = jnp.sum(pr, axis=-1, keepdims=True)
            probs.append(pr.astype(BF16))
            dens.append(den)
            lse_tile = jnp.where(lane == h, m + jnp.log(den), lse_tile)
        for hp in range(DIL_HEADS // 2):
            cols = slice(hp * LANES, (hp + 1) * LANES)
            vw = vpad_ref[pl.ds(ks, W), cols]
            outs = [jnp.dot(probs[2 * hp + e], vw, preferred_element_type=F32) / dens[2 * hp + e] for e in range(2)]
            o_ref[0, pl.ds(qs, TQ), cols] = jnp.where(low, outs[0], outs[1])
        lse_ref[0, pl.ds(qs, TQ), :] = lse_tile
        return 0

    lax.fori_loop(0, LQ // TQ, tile, 0)


def dilated_group(pg, tab, qkv_col, dilation, lq=2048):
    B, r, L, _ = pg.shape
    assert r == dilation and L >= 2 * DIL_TQ and L % DIL_TQ == 0
    LQ = min(lq, L)

    def col(which, rows, tiled):
        cb = qkv_col // DIL_COLS + which
        return pl.BlockSpec((1, None, rows, DIL_COLS), lambda b, rho, t: (b, rho, t if tiled else 0, cb))

    return pl.pallas_call(
        functools.partial(_dil_kernel, L=L, LQ=LQ, dilation=r),
        grid=(B, r, L // LQ),
        in_specs=[pl.BlockSpec(memory_space=pltpu.SMEM),
                  col(0, LQ, True), col(1, L, False), col(2, L, False)],
        out_specs=[pl.BlockSpec((1, None, LQ, DIL_COLS), lambda b, rho, t: (b, rho, t, 0)),
                   pl.BlockSpec((1, None, LQ, LANES), lambda b, rho, t: (b, rho, t, 0))],
        out_shape=[jax.ShapeDtypeStruct((B, r, L, DIL_COLS), F32),
                   jax.ShapeDtypeStruct((B, r, L, LANES), F32)],
        scratch_shapes=[pltpu.VMEM((L + 2 * DIL_PAD, DIL_COLS), BF16),
                        pltpu.VMEM((L + 2 * DIL_PAD, DIL_COLS), BF16),
                        pltpu.VMEM((3, DIL_HEADS, DIL_TQ, DIL_WIN), F32)],
        compiler_params=_cparams(("arbitrary", "arbitrary", "arbitrary")),
        name=f"dilated_r{r}",
    )(tab, pg, pg, pg)


def _split_dot(w, e_ref):
    hi = w.astype(BF16)
    lo = (w - hi.astype(F32)).astype(BF16)
    e = e_ref[...]
    return jnp.dot(hi, e, preferred_element_type=F32) + jnp.dot(lo, e, preferred_element_type=F32)


def _token_order(src_ref, tmp_ref, r):
    if r == 1:
        return src_ref[0, 0]
    n, width = src_ref.shape[2], src_ref.shape[3]
    for c in range(width // LANES):
        for rho in range(r):
            tmp_ref[c, pl.ds(rho, n, stride=r), :] = src_ref[0, rho, :, c * LANES:(c + 1) * LANES]
    return jnp.concatenate([tmp_ref[c] for c in range(width // LANES)], axis=1)


def _merge_kernel(x_ref, mod_ref, oa_ref, o0_ref, o1_ref, o2_ref, l0_ref, l1_ref, l2_ref,
                  ga0_ref, ga1_ref, gb0_ref, gb1_ref, e_ref, wpa_ref, wpb_ref, wout_ref, gffn_ref,
                  x1_ref, h2_ref, to1_ref, to2_ref, tl1_ref, tl2_ref):
    dil = [d for _, d in DIL_GROUPS]
    a = jnp.dot(oa_ref[0], wpa_ref[...], preferred_element_type=F32)
    l0 = _token_order(l0_ref, None, dil[0])
    l1 = _token_order(l1_ref, tl1_ref, dil[1])
    l2 = _token_order(l2_ref, tl2_ref, dil[2])
    mx = jnp.maximum(jnp.maximum(l0, l1), l2)
    e0, e1, e2 = jnp.exp(l0 - mx), jnp.exp(l1 - mx), jnp.exp(l2 - mx)
    den = e0 + e1 + e2
    ob = (_token_order(o0_ref, None, dil[0]) * _split_dot(e0 / den, e_ref)
          + _token_order(o1_ref, to1_ref, dil[1]) * _split_dot(e1 / den, e_ref)
          + _token_order(o2_ref, to2_ref, dil[2]) * _split_dot(e2 / den, e_ref))
    b = jnp.dot(ob.astype(BF16), wpb_ref[...], preferred_element_type=F32)
    ga = jnp.concatenate([ga0_ref[0], ga1_ref[0]], axis=-1).astype(F32)
    gb = jnp.concatenate([gb0_ref[0], gb1_ref[0]], axis=-1).astype(F32)
    merged = a / (1.0 + jnp.exp(-ga)) + b / (1.0 + jnp.exp(-gb))
    mix = jnp.dot(merged.astype(BF16), wout_ref[...], preferred_element_type=F32)
    x1 = x_ref[0] + mod_ref[0, 2:3, :] * mix
    x1_ref[0] = x1
    h2_ref[0] = _modulated_norm(x1, gffn_ref[...], mod_ref[0, 4:5, :], mod_ref[0, 3:4, :]).astype(BF16)


def merge_proj(x, mod, p_nat, oa, outs, lses, wpa, wpb, wout, g_ffn, tm=512):
    B, S, _ = x.shape
    tm = min(tm, S)
    expand = np.zeros((LANES, DIL_COLS), np.float32)
    for h in range(DIL_HEADS):
        expand[h, h * DIL_HEAD_DIM:(h + 1) * DIL_HEAD_DIM] = 1.0
    expand = jnp.asarray(expand, BF16)

    def tok(w):
        return pl.BlockSpec((1, tm, w), lambda b, i: (b, i, 0))

    def res(w, r):
        return pl.BlockSpec((1, r, tm // r, w), lambda b, i: (b, 0, i, 0))

    def pcol(off):
        return pl.BlockSpec((1, tm, DIL_COLS), lambda b, i: (b, i, off // DIL_COLS))

    def whole(a):
        return pl.BlockSpec(a.shape, lambda b, i: (0,) * a.ndim)

    g = g_ffn.reshape(1, -1)
    dil = [d for _, d in DIL_GROUPS]
    return pl.pallas_call(
        _merge_kernel,
        grid=(B, S // tm),
        in_specs=[tok(D_MODEL), pl.BlockSpec((1, 6, D_MODEL), lambda b, i: (b, 0, 0)), tok(DA_V),
                  res(DIL_COLS, dil[0]), res(DIL_COLS, dil[1]), res(DIL_COLS, dil[2]),
                  res(LANES, dil[0]), res(LANES, dil[1]), res(LANES, dil[2]),
                  pcol(NAT_GA), pcol(NAT_GA + DIL_COLS), pcol(NAT_GB), pcol(NAT_GB + DIL_COLS),
                  whole(expand), whole(wpa), whole(wpb), whole(wout), whole(g)],
        out_specs=[tok(D_MODEL), tok(D_MODEL)],
        out_shape=[jax.ShapeDtypeStruct((B, S, D_MODEL), F32), jax.ShapeDtypeStruct((B, S, D_MODEL), BF16)],
        scratch_shapes=[pltpu.VMEM((DIL_COLS // LANES, tm, LANES), F32),
                        pltpu.VMEM((DIL_COLS // LANES, tm, LANES), F32),
                        pltpu.VMEM((1, tm, LANES), F32), pltpu.VMEM((1, tm, LANES), F32)],
        compiler_params=_cparams(("arbitrary", "arbitrary")),
        name="merge_proj",
    )(x, mod, oa, outs[0], outs[1], outs[2], lses[0], lses[1], lses[2], p_nat, p_nat, p_nat, p_nat,
      expand, wpa, wpb, wout, g)


def _top_values(s, count, ranks=False):
    row = lax.broadcasted_iota(jnp.int32, (count, s.shape[1]), 0)
    rows = []
    top = jnp.zeros((count, s.shape[1]), F32)
    rank = jnp.full(s.shape, float(count), F32)
    for i in range(count):
        m = jnp.max(s, axis=0, keepdims=True)
        rows.append(m)
        top = jnp.where(row == i, m, top)
        hit = s >= m
        if ranks:
            rank = jnp.where(hit, float(i), rank)
        s = jnp.where(hit, -jnp.inf, s)
    return (rows, top, rank) if ranks else (rows, top)


def _paired_bf16(x):
    bits = pltpu.bitcast(x.astype(BF16).astype(F32), jnp.uint32)
    return bits | (bits >> 16)


def _route_kernel(h_ref, wq_ref, k1_ref, k2_ref, r2_ref, e2_ref, cnt_ref, e1_ref):
    qp = jnp.dot(h_ref[...], wq_ref[...], preferred_element_type=F32)
    tm = qp.shape[0]
    K, half = PEER_TOPK, PEER_TOPK // 2
    first_row = lax.broadcasted_iota(jnp.int32, (K, tm), 0) == 0
    for h in range(PEER_HEADS):
        q1 = qp[:, h * PEER_QUERY_DIM:h * PEER_QUERY_DIM + PEER_HALF].astype(BF16)
        q2 = qp[:, h * PEER_QUERY_DIM + PEER_HALF:(h + 1) * PEER_QUERY_DIM].astype(BF16)
        s1 = _dot_nt(k1_ref[h], q1)
        s2 = _dot_nt(k2_ref[h], q2)
        r1, v1 = _top_values(s1, K)
        r2, v2, rank2 = _top_values(s2, K, ranks=True)
        pieces = [v1 + r2[0]] + [v1[0:half] + r2[j] for j in range(1, half)] + [v2[half:K] + r1[0]]
        cand = jnp.concatenate(pieces, axis=0)
        tau = _top_values(cand, K)[0][K - 1]
        top = r1[0] + r2[0]
        z = jnp.sum(jnp.where(cand >= tau, jnp.exp(cand - top), 0.0), axis=0, keepdims=True)
        reach = [jnp.where(p >= tau, 1.0, 0.0) for p in pieces]
        low = reach[1]
        for p in reach[2:half]:
            low = low + p
        tail = jnp.sum(reach[half], axis=0, keepdims=True)
        cnt_rank = reach[0] + jnp.concatenate([low, jnp.zeros_like(low)], axis=0) + jnp.where(first_row, tail, 0.0)
        cnt = jnp.zeros(s1.shape, F32)
        for i in range(K):
            cnt = jnp.where(s1 == r1[i], cnt_rank[i:i + 1], cnt)
        r2_ref[h] = rank2.astype(BF16)
        e2_ref[h] = (jnp.exp(s2 - r2[0]) / z).astype(BF16)
        cnt_ref[h] = pltpu.bitcast(_paired_bf16(cnt), BF16)
        e1_ref[h] = pltpu.bitcast(_paired_bf16(0.5 * jnp.exp(s1 - r1[0])), BF16)


def peer_route(h2, wq, k1, k2, tm=256):
    T = h2.shape[0]
    tm = min(tm, T)
    half = jax.ShapeDtypeStruct((PEER_HEADS, N_KEYS, T), BF16)
    word = jax.ShapeDtypeStruct((PEER_HEADS, 2 * N_KEYS, T), BF16)
    wspec = pl.BlockSpec((PEER_HEADS, 2 * N_KEYS, tm), lambda i: (0, 0, i))
    bspec = pl.BlockSpec((PEER_HEADS, N_KEYS, tm), lambda i: (0, 0, i))
    kspec = pl.BlockSpec((PEER_HEADS, N_KEYS, PEER_HALF), lambda i: (0, 0, 0))
    return pl.pallas_call(
        _route_kernel,
        grid=(T // tm,),
        in_specs=[pl.BlockSpec((tm, D_MODEL), lambda i: (i, 0)),
                  pl.BlockSpec(wq.shape, lambda i: (0, 0)), kspec, kspec],
        out_specs=[bspec, bspec, wspec, wspec],
        out_shape=[half, half, word, word],
        compiler_params=_cparams(("arbitrary",)),
        name="peer_route",
    )(h2, wq, k1, k2)


PEER_TE = 2048
PEER_SLAB_GROUP = 8
PACK = 16


def _dense_kernel(h_ref, u_ref, vt_ref, r2_ref, e2_ref, cnt_ref, e1_ref, x1_ref, mod_ref, gfin_ref,
                  y_ref, acc_ref, a_ref):
    j = pl.program_id(1)
    tm = h_ref.shape[0]

    @pl.when(j == 0)
    def _():
        acc_ref[...] = jnp.zeros_like(acc_ref)

    def rows_bf16(ref, h, first):
        packed = ref[h, pl.ds(pl.multiple_of(2 * first, PACK), PACK), :]
        return [jnp.broadcast_to(packed[2 * al:2 * al + 1, :], (PACK, tm)) for al in range(PEER_SLAB_GROUP)]

    zero = jnp.zeros((PACK, tm), BF16)
    group_rows = PEER_SLAB_GROUP * N_KEYS
    n_groups = PEER_TE // group_rows
    zts = [_dot_nt(u_ref[g * group_rows:(g + 1) * group_rows, :], h_ref[...]) for g in range(n_groups)]
    for sg in range(n_groups):
        experts = slice(sg * group_rows, (sg + 1) * group_rows)
        zt = zts[sg]
        first = pl.multiple_of(j * (PEER_TE // N_KEYS) + sg * PEER_SLAB_GROUP, PEER_SLAB_GROUP)
        cnt_rows = [rows_bf16(cnt_ref, h, first) for h in range(PEER_HEADS)]
        e1_rows = [rows_bf16(e1_ref, h, first) for h in range(PEER_HEADS)]
        for c in range(N_KEYS // PACK):
            keys = slice(c * PACK, (c + 1) * PACK)
            w = [None] * PEER_SLAB_GROUP
            for h in range(PEER_HEADS):
                r2 = r2_ref[h, keys, :]
                e2 = e2_ref[h, keys, :]
                for al in range(PEER_SLAB_GROUP):
                    term = jnp.where(r2 < cnt_rows[h][al], e2, zero) * e1_rows[h][al]
                    w[al] = term if w[al] is None else w[al] + term
            for al in range(PEER_SLAB_GROUP):
                base = al * N_KEYS + c * PACK
                z = zt[base:base + PACK, :]
                a_ref[sg * group_rows + base:sg * group_rows + base + PACK, :] = (
                    (z * (1.0 + lax.erf(z * (2.0 ** -0.5)))).astype(BF16) * w[al])
        acc_ref[...] += jnp.dot(vt_ref[:, experts], a_ref[experts, :], preferred_element_type=F32)

    @pl.when(j == pl.num_programs(1) - 1)
    def _():
        x2 = x1_ref[...] + mod_ref[0, 5:6, :] * acc_ref[...].T
        ms = jnp.mean(x2 * x2, axis=-1, keepdims=True)
        y_ref[...] = x2 * lax.rsqrt(ms + EPS) * gfin_ref[...]


def peer_dense(h2, u_bf16, vt_bf16, routes, x1, mod, g_final, S, tm=512):
    T = h2.shape[0]
    tm = min(tm, S)
    r2, e2, cnt, e1 = routes
    rspec = pl.BlockSpec((PEER_HEADS, N_KEYS, tm), lambda i, j: (0, 0, i))
    wspec = pl.BlockSpec((PEER_HEADS, 2 * N_KEYS, tm), lambda i, j: (0, 0, i))
    tiles_per_row = S // tm
    return pl.pallas_call(
        _dense_kernel,
        grid=(T // tm, N_EXPERTS // PEER_TE),
        in_specs=[pl.BlockSpec((tm, D_MODEL), lambda i, j: (i, 0)),
                  pl.BlockSpec((PEER_TE, D_MODEL), lambda i, j: (j, 0)),
                  pl.BlockSpec((D_MODEL, PEER_TE), lambda i, j: (0, j)),
                  rspec, rspec, wspec, wspec,
                  pl.BlockSpec((tm, D_MODEL), lambda i, j: (i, 0)),
                  pl.BlockSpec((1, 6, D_MODEL), lambda i, j: (i // tiles_per_row, 0, 0)),
                  pl.BlockSpec((1, D_MODEL), lambda i, j: (0, 0))],
        out_specs=pl.BlockSpec((tm, D_MODEL), lambda i, j: (i, 0)),
        out_shape=jax.ShapeDtypeStruct((T, D_MODEL), F32),
        scratch_shapes=[pltpu.VMEM((D_MODEL, tm), F32), pltpu.VMEM((PEER_TE, tm), BF16)],
        compiler_params=_cparams(("arbitrary", "arbitrary")),
        name="peer_dense",
    )(h2, u_bf16, vt_bf16, r2, e2, cnt, e1, x1, mod, g_final.reshape(1, -1))


def _prepare_weights(w_in, w_proj_a, w_proj_b, w_out, rel_bias, w_query, sub_keys_1, sub_keys_2,
                     expert_u, expert_v):
    tabs = [rel_bias[:, :DA_HEADS].reshape(-1)]
    for gi in range(len(DIL_GROUPS)):
        lo = DA_HEADS + gi * DIL_HEADS
        tabs.append(rel_bias[:, lo:lo + DIL_HEADS].reshape(-1))
    w = w_in[0].astype(BF16)

    def group_cols(gi):
        return [w[:, off + gi * DIL_COLS:off + (gi + 1) * DIL_COLS] for off in (OFF_QB, OFF_KB, OFF_VB)]

    w_nat = jnp.concatenate([w[:, :OFF_QB]] + group_cols(0) + [w[:, OFF_GA:]], axis=1)
    w_groups = [jnp.concatenate(group_cols(gi), axis=1) for gi in range(1, len(DIL_GROUPS))]
    return dict(
        w_nat=w_nat, w_groups=w_groups, wpa=w_proj_a[0].astype(BF16), wpb=w_proj_b[0].astype(BF16),
        wout=w_out[0].astype(BF16), tabs=tabs, wq=w_query[0].astype(BF16),
        k1=sub_keys_1[0].astype(BF16), k2=sub_keys_2[0].astype(BF16),
        u=expert_u[0].astype(BF16), vt=expert_v[0].astype(BF16).T)


def _encoder(x, c, w_ada, b_ada, g_norm_mix, g_norm_ffn, lq1, lk1, lq2, lk2, g_subln, g_final, wts):
    B, S, _ = x.shape
    mod = ada_mod(c, w_ada[0], b_ada[0]).reshape(B, 6, D_MODEL)
    p_nat = in_proj(x, mod, g_norm_mix[0], wts["w_nat"], tn=NAT_COLS // 4)
    oa = diff_attn(p_nat, wts["tabs"][0], lq1[0], lk1[0], lq2[0], lk2[0], g_subln[0])
    outs, lses = [], []
    for gi, (_, dilation) in enumerate(DIL_GROUPS):
        if gi == 0:
            pg, qkv_col = p_nat.reshape(B, 1, S, NAT_COLS), NAT_Q0
        else:
            pg, qkv_col = in_proj(x, mod, g_norm_mix[0], wts["w_groups"][gi - 1], r=dilation), 0
        o_g, lse_g = dilated_group(pg, wts["tabs"][1 + gi], qkv_col, dilation)
        outs.append(o_g)
        lses.append(lse_g)
    x1, h2 = merge_proj(x, mod, p_nat, oa, outs, lses, wts["wpa"], wts["wpb"], wts["wout"], g_norm_ffn[0])
    h2 = h2.reshape(B * S, D_MODEL)
    routes = peer_route(h2, wts["wq"], wts["k1"], wts["k2"])
    y = peer_dense(h2, wts["u"], wts["vt"], routes, x1.reshape(B * S, D_MODEL), mod, g_final, S)
    return y.reshape(B, S, D_MODEL)


def kernel(x_prompt, x_sample, c_prompt, c_sample, w_ada, b_ada, g_norm_mix, g_norm_ffn, w_in, w_proj_a, w_proj_b, w_out, lambda_q1, lambda_k1, lambda_q2, lambda_k2, g_subln, rel_bias, w_query, sub_keys_1, sub_keys_2, expert_u, expert_v, g_final):
    wts = _prepare_weights(w_in, w_proj_a, w_proj_b, w_out, rel_bias, w_query, sub_keys_1, sub_keys_2,
                           expert_u, expert_v)
    args = (w_ada, b_ada, g_norm_mix, g_norm_ffn, lambda_q1, lambda_k1, lambda_q2, lambda_k2, g_subln, g_final, wts)
    return _encoder(x_prompt, c_prompt, *args), _encoder(x_sample, c_sample, *args)
```
